```python
import jax, jax.numpy as jnp
from jax import lax
import numpy as np

D_MODEL = 1024
BATCH = 2
SEQ = 8192
DEPTH = 4
DEC_BATCH = 32
DEC_SEQ = 32
PAST_LEN = 4096

CHUNK = 64
N_MIXERS = 2
N_CONV_LAYERS = (DEPTH + 1) // 2
N_MLSTM_LAYERS = DEPTH // 2
CONV_WIDTH = 31
CONV_HIST = CONV_WIDTH - 1
MLSTM_INNER = 2 * D_MODEL
MLSTM_HEADS = 4
MLSTM_HEAD_DIM = MLSTM_INNER // MLSTM_HEADS
MLSTM_QKV_BLOCK = 4
MLSTM_CONV_WIDTH = 4
MLSTM_CONV_HIST = MLSTM_CONV_WIDTH - 1
PEER_HEADS = 8
PEER_N_KEYS = 128
PEER_N_EXPERTS = PEER_N_KEYS * PEER_N_KEYS
PEER_TOPK = 16
PEER_QUERY_DIM = 256
PEER_HALF = PEER_QUERY_DIM // 2
PEER_TOKEN_BLOCK = 256
RMS_EPS = 1e-6
LN_EPS = 1e-5

kernel_name = 'hybrid_conformer_mlstm_peer_stream'

F32 = jnp.float32


def rmsnorm(x, g):
    xf = x.astype(F32)
    y = xf * lax.rsqrt(jnp.mean(xf * xf, axis=-1, keepdims=True) + RMS_EPS)
    return (y * g.astype(F32)).astype(x.dtype)


def layernorm(x, g, b):
    xf = x.astype(F32)
    mu = jnp.mean(xf, axis=-1, keepdims=True)
    var = jnp.mean(jnp.square(xf - mu), axis=-1, keepdims=True)
    return ((xf - mu) * lax.rsqrt(var + LN_EPS) * g.astype(F32) + b.astype(F32)).astype(x.dtype)


def causal_dwconv(x, hist, w, b):
    xp = jnp.concatenate([hist.astype(x.dtype), x], axis=1)
    y = lax.conv_general_dilated(xp, w[:, None, :].astype(x.dtype), window_strides=(1,), padding='VALID',
                                 dimension_numbers=('NWC', 'WIO', 'NWC'), feature_group_count=x.shape[-1])
    return y + b.astype(x.dtype), xp[:, -hist.shape[1]:]


def conv_module(h, hist, w1, b1, dw, dwb, ln_g, ln_b, w2, b2):
    a = h @ w1 + b1
    u = a[..., :D_MODEL] * jax.nn.sigmoid(a[..., D_MODEL:])
    c, new_hist = causal_dwconv(u, hist, dw, dwb)
    c = jax.nn.silu(layernorm(c, ln_g, ln_b))
    return c @ w2 + b2, new_hist


def blockdiag(x, w):
    xs = x.reshape(x.shape[:-1] + (w.shape[0], w.shape[1]))
    return jnp.einsum('...gi,gij->...gj', xs, w).reshape(x.shape)


def mlstm_block(state, inp):
    C, n, m = state
    q, k, v, ig, lf = inp
    L = q.shape[1]
    b = jnp.swapaxes(jnp.cumsum(lf, axis=1), 1, 2)
    igt = jnp.swapaxes(ig, 1, 2)
    causal = jnp.tril(jnp.ones((L, L), dtype=bool))
    logw = jnp.where(causal, b[..., :, None] - b[..., None, :] + igt[..., None, :], -jnp.inf)
    inter = b + m[..., None]
    m_row = jnp.maximum(jnp.max(logw, axis=-1), inter)
    w_intra = jnp.exp(logw - m_row[..., None])
    w_inter = jnp.swapaxes(jnp.exp(inter - m_row), 1, 2)[..., None]
    s = jnp.einsum('blhd,bshd->bhls', q, k) * w_intra
    num = jnp.einsum('bhls,bshd->blhd', s, v) + w_inter * jnp.einsum('blhd,bhde->blhe', q, C)
    den = jnp.swapaxes(jnp.sum(s, axis=-1), 1, 2)[..., None] + w_inter * jnp.einsum('blhd,bhd->blh', q, n)[..., None]
    floor = jnp.swapaxes(jnp.exp(-m_row), 1, 2)[..., None]
    h = num / jnp.maximum(jnp.abs(den), floor)
    b_end = b[..., -1]
    g = b_end[..., None] - b + igt
    m_new = jnp.maximum(b_end + m, jnp.max(g, axis=-1))
    wk = jnp.exp(g - m_new[..., None])
    decay = jnp.exp(b_end + m - m_new)
    C_new = decay[..., None, None] * C + jnp.einsum('bhs,bshd,bshe->bhde', wk, k, v)
    n_new = decay[..., None] * n + jnp.einsum('bhs,bshd->bhd', wk, k)
    return (C_new, n_new, m_new), h


def mlstm_scan(q, k, v, ig, lf, C, n, m, block):
    B, T, H, DH = q.shape
    nb = T // block
    def to_blocks(t):
        return jnp.moveaxis(t.reshape((B, nb, block) + t.shape[2:]), 1, 0)
    xs = (to_blocks(q), to_blocks(k), to_blocks(v), to_blocks(ig), to_blocks(lf))
    state, hs = lax.scan(mlstm_block, (C.astype(F32), n.astype(F32), m.astype(F32)), xs)
    return jnp.moveaxis(hs, 0, 1).reshape(B, T, H, DH), state


def mlstm_mixer(h, conv_hist, C, n, m, block, wup, convw, convb, wq, wk, wv, wi, bi, wf, bf, norm_g, skip, wdown):
    B, T, _ = h.shape
    up = h @ wup
    xm, z = up[..., :MLSTM_INNER], up[..., MLSTM_INNER:]
    xc, new_hist = causal_dwconv(xm, conv_hist, convw, convb)
    xc = jax.nn.silu(xc)
    q = blockdiag(xc, wq)
    k = blockdiag(xc, wk)
    v = blockdiag(xm, wv)
    qkv = jnp.concatenate([q, k, v], axis=-1)
    ig = (qkv @ wi + bi).astype(F32)
    lf = jax.nn.log_sigmoid((qkv @ wf + bf).astype(F32))
    split = lambda t: t.reshape(B, T, MLSTM_HEADS, MLSTM_HEAD_DIM).astype(F32)
    hh, (C2, n2, m2) = mlstm_scan(split(q), split(k) * (MLSTM_HEAD_DIM ** -0.5), split(v), ig, lf, C, n, m, block)
    mu = jnp.mean(hh, axis=-1, keepdims=True)
    var = jnp.mean(jnp.square(hh - mu), axis=-1, keepdims=True)
    hn = (hh - mu) * lax.rsqrt(var + LN_EPS) * norm_g.astype(F32).reshape(MLSTM_HEADS, MLSTM_HEAD_DIM)
    hn = hn.reshape(B, T, MLSTM_INNER).astype(h.dtype)
    out = jax.nn.sigmoid(z) * (hn + skip * xc)
    return out @ wdown, new_hist, C2, n2, m2


def peer_block(xb, wq, k1, k2, u_tab, v_tab):
    T = xb.shape[0]
    q = (xb @ wq).reshape(T, PEER_HEADS, 2, PEER_HALF).astype(F32)
    s1 = jnp.einsum('thd,hnd->thn', q[:, :, 0], k1.astype(F32))
    s2 = jnp.einsum('thd,hnd->thn', q[:, :, 1], k2.astype(F32))
    v1, i1 = lax.top_k(s1, PEER_TOPK)
    v2, i2 = lax.top_k(s2, PEER_TOPK)
    cand = (v1[..., :, None] + v2[..., None, :]).reshape(T, PEER_HEADS, PEER_TOPK * PEER_TOPK)
    vals, idx = lax.top_k(cand, PEER_TOPK)
    e1 = jnp.take_along_axis(i1, idx // PEER_TOPK, axis=-1)
    e2 = jnp.take_along_axis(i2, idx % PEER_TOPK, axis=-1)
    experts = e1 * PEER_N_KEYS + e2
    gates = jax.nn.softmax(vals, axis=-1)
    act = jax.nn.gelu(jnp.einsum('td,thkd->thk', xb, u_tab[experts]).astype(F32))
    w = (gates * act).astype(xb.dtype)
    return jnp.einsum('thk,thkd->td', w, v_tab[experts])


def peer(x, wq, k1, k2, u_tab, v_tab):
    B, T, D = x.shape
    n = B * T
    nblk = -(-n // PEER_TOKEN_BLOCK)
    flat = jnp.pad(x.reshape(n, D), ((0, nblk * PEER_TOKEN_BLOCK - n), (0, 0)))
    out = lax.map(lambda xb: peer_block(xb, wq, k1, k2, u_tab, v_tab), flat.reshape(nblk, PEER_TOKEN_BLOCK, D))
    return out.reshape(nblk * PEER_TOKEN_BLOCK, D)[:n].reshape(B, T, D)


def setup_inputs(seed: int = 0) -> dict:
    key = jax.random.key(seed)
    ks = iter(jax.random.split(key, 64))
    def nrm(shape, scale):
        return jax.random.normal(next(ks), shape, F32) * scale
    D, DI, H, DH = D_MODEL, MLSTM_INNER, MLSTM_HEADS, MLSTM_HEAD_DIM
    NA, NB = N_CONV_LAYERS, N_MLSTM_LAYERS
    G = DI // MLSTM_QKV_BLOCK
    return {
        'x_prompt': nrm((BATCH, SEQ, D), 1.0),
        'x_sample': nrm((DEC_BATCH, DEC_SEQ, D), 1.0),
        'cache_conv': nrm((NA, DEC_BATCH, CONV_HIST, D), 0.5),
        'state_mlstm_conv': nrm((NB, DEC_BATCH, MLSTM_CONV_HIST, DI), 1.0),
        'state_C': nrm((NB, DEC_BATCH, H, DH, DH), 0.05),
        'state_n': jnp.abs(nrm((NB, DEC_BATCH, H, DH), 0.05)),
        'state_m': nrm((NB, DEC_BATCH, H), 1.0),
        'norm_mix': 1.0 + nrm((DEPTH, D), 0.02),
        'norm_ffn': 1.0 + nrm((DEPTH, D), 0.02),
        'norm_final': 1.0 + nrm((D,), 0.02),
        'cm_w1': nrm((NA, D, 2 * D), D ** -0.5),
        'cm_b1': nrm((NA, 2 * D), 0.02),
        'cm_dw': nrm((NA, CONV_WIDTH, D), CONV_WIDTH ** -0.5),
        'cm_dwb': nrm((NA, D), 0.02),
        'cm_ln_g': 1.0 + nrm((NA, D), 0.02),
        'cm_ln_b': nrm((NA, D), 0.02),
        'cm_w2': nrm((NA, D, D), D ** -0.5),
        'cm_b2': nrm((NA, D), 0.02),
        'ml_wup': nrm((NB, D, 2 * DI), D ** -0.5),
        'ml_convw': nrm((NB, MLSTM_CONV_WIDTH, DI), MLSTM_CONV_WIDTH ** -0.5),
        'ml_convb': nrm((NB, DI), 0.02),
        'ml_wq': nrm((NB, G, MLSTM_QKV_BLOCK, MLSTM_QKV_BLOCK), MLSTM_QKV_BLOCK ** -0.5),
        'ml_wk': nrm((NB, G, MLSTM_QKV_BLOCK, MLSTM_QKV_BLOCK), MLSTM_QKV_BLOCK ** -0.5),
        'ml_wv': nrm((NB, G, MLSTM_QKV_BLOCK, MLSTM_QKV_BLOCK), MLSTM_QKV_BLOCK ** -0.5),
        'ml_wi': nrm((NB, 3 * DI, H), 0.1 * (3 * DI) ** -0.5),
        'ml_bi': nrm((NB, H), 0.1),
        'ml_wf': nrm((NB, 3 * DI, H), 0.1 * (3 * DI) ** -0.5),
        'ml_bf': jnp.linspace(3.0, 6.0, H, dtype=F32)[None, :] + nrm((NB, H), 0.01),
        'ml_norm': 1.0 + nrm((NB, DI), 0.02),
        'ml_skip': 1.0 + nrm((NB, DI), 0.02),
        'ml_wdown': nrm((NB, DI, D), DI ** -0.5),
        'pk_wq': nrm((DEPTH, D, PEER_HEADS * PEER_QUERY_DIM), D ** -0.5),
        'pk_k1': nrm((DEPTH, PEER_HEADS, PEER_N_KEYS, PEER_HALF), PEER_HALF ** -0.5),
        'pk_k2': nrm((DEPTH, PEER_HEADS, PEER_N_KEYS, PEER_HALF), PEER_HALF ** -0.5),
        'pk_u': nrm((DEPTH, PEER_N_EXPERTS, D), D ** -0.5),
        'pk_v': nrm((DEPTH, PEER_N_EXPERTS, D), (PEER_HEADS * PEER_TOPK) ** -0.5),
    }


def reference(x_prompt, x_sample, cache_conv, state_mlstm_conv, state_C, state_n, state_m,
              norm_mix, norm_ffn, norm_final,
              cm_w1, cm_b1, cm_dw, cm_dwb, cm_ln_g, cm_ln_b, cm_w2, cm_b2,
              ml_wup, ml_convw, ml_convb, ml_wq, ml_wk, ml_wv, ml_wi, ml_bi, ml_wf, ml_bf,
              ml_norm, ml_skip, ml_wdown,
              pk_wq, pk_k1, pk_k2, pk_u, pk_v):
    xp, xs = x_prompt, x_sample
    bp, bs = xp.shape[0], xs.shape[0]
    conv_p, conv_s, mconv_p, mconv_s = [], [], [], []
    Cp_l, Cs_l, np_l, ns_l, mp_l, ms_l = [], [], [], [], [], []
    for layer in range(DEPTH):
        j = layer // N_MIXERS
        hp = rmsnorm(xp, norm_mix[layer])
        hs = rmsnorm(xs, norm_mix[layer])
        if layer % N_MIXERS == 0:
            cw = (cm_w1[j], cm_b1[j], cm_dw[j], cm_dwb[j], cm_ln_g[j], cm_ln_b[j], cm_w2[j], cm_b2[j])
            yp, hist_p = conv_module(hp, jnp.zeros((bp, CONV_HIST, D_MODEL), xp.dtype), *cw)
            ys, hist_s = conv_module(hs, cache_conv[j], *cw)
            conv_p.append(hist_p)
            conv_s.append(hist_s)
        else:
            mw = (ml_wup[j], ml_convw[j], ml_convb[j], ml_wq[j], ml_wk[j], ml_wv[j], ml_wi[j], ml_bi[j],
                  ml_wf[j], ml_bf[j], ml_norm[j], ml_skip[j], ml_wdown[j])
            yp, hcp, Cp, np_, mp = mlstm_mixer(
                hp, jnp.zeros((bp, MLSTM_CONV_HIST, MLSTM_INNER), xp.dtype),
                jnp.zeros((bp, MLSTM_HEADS, MLSTM_HEAD_DIM, MLSTM_HEAD_DIM), F32),
                jnp.zeros((bp, MLSTM_HEADS, MLSTM_HEAD_DIM), F32),
                jnp.zeros((bp, MLSTM_HEADS), F32), CHUNK, *mw)
            ys, hcs, Cs, ns_, ms = mlstm_mixer(
                hs, state_mlstm_conv[j], state_C[j], state_n[j], state_m[j], xs.shape[1], *mw)
            mconv_p.append(hcp)
            mconv_s.append(hcs)
            Cp_l.append(Cp.astype(xp.dtype))
            Cs_l.append(Cs.astype(state_C.dtype))
            np_l.append(np_.astype(xp.dtype))
            ns_l.append(ns_.astype(state_n.dtype))
            mp_l.append(mp.astype(xp.dtype))
            ms_l.append(ms.astype(state_m.dtype))
        xp = xp + yp
        xs = xs + ys
        pw = (pk_wq[layer], pk_k1[layer], pk_k2[layer], pk_u[layer], pk_v[layer])
        xp = xp + peer(rmsnorm(xp, norm_ffn[layer]), *pw)
        xs = xs + peer(rmsnorm(xs, norm_ffn[layer]), *pw)
    y_prompt = rmsnorm(xp, norm_final)
    y_sample = rmsnorm(xs, norm_final)
    return (y_prompt, y_sample,
            jnp.stack(conv_p), jnp.stack(conv_s),
            jnp.stack(mconv_p), jnp.stack(mconv_s),
            jnp.stack(Cp_l), jnp.stack(Cs_l),
            jnp.stack(np_l), jnp.stack(ns_l),
            jnp.stack(mp_l), jnp.stack(ms_l))
```

```python
import functools

import jax
import jax.numpy as jnp
from jax import lax
from jax.experimental import pallas as pl
from jax.experimental.pallas import tpu as pltpu

F32 = jnp.float32
MXU_DT = jnp.bfloat16

D_MODEL = 1024
DEPTH = 4
N_MIXERS = 2
CONV_WIDTH = 31
CONV_HIST = CONV_WIDTH - 1
CONV_PAD = 32
MLSTM_INNER = 2 * D_MODEL
MLSTM_HEADS = 4
MLSTM_HEAD_DIM = MLSTM_INNER // MLSTM_HEADS
MLSTM_QKV_BLOCK = 4
MLSTM_CONV_WIDTH = 4
MLSTM_CONV_HIST = MLSTM_CONV_WIDTH - 1
MLSTM_CONV_PAD = 8
MLSTM_PROMPT_CHUNK = 256
PEER_HEADS = 8
PEER_N_KEYS = 128
PEER_N_EXPERTS = PEER_N_KEYS * PEER_N_KEYS
PEER_TOPK = 16
PEER_HALF = 128
RMS_EPS = 1e-6
LN_EPS = 1e-5

LANES = 128
SUBLANES = 8
MXU_TILE = 256
GATE_ROWS = 16
VMEM_LIMIT = 56 * 1024 * 1024
NEG_INF = float("-inf")
POS_INF = float("inf")


def _full(shape):
    n = len(shape)
    return pl.BlockSpec(shape, lambda *_: (0,) * n)


def _rms(x, g):
    return x * lax.rsqrt(jnp.mean(x * x, axis=-1, keepdims=True) + RMS_EPS) * g


def _mm(a, b):
    return jnp.dot(a.astype(MXU_DT), b.astype(MXU_DT), preferred_element_type=F32)


def _mm_nt(a, b):
    return lax.dot_general(a.astype(MXU_DT), b.astype(MXU_DT), (((1,), (1,)), ((), ())),
                           preferred_element_type=F32)


CONV_ROW_BLOCK = 32


def _conv_kernel(x_ref, hist_ref, g_ref, w1_ref, b1_ref, dw_ref, dwb_ref, lng_ref, lnb_ref, w2_ref, b2_ref,
                 y_ref, nh_ref, ubuf, cbuf, *, bb, tt):
    d = D_MODEL
    t = pl.program_id(1)

    @pl.when(t == 0)
    def _():
        ubuf[:, 0:CONV_PAD, :] = hist_ref[...]

    x = x_ref[...].reshape(bb * tt, d)
    h = _rms(x, g_ref[...])
    a = _mm(h, w1_ref[...]) + b1_ref[...]
    u = a[:, :d] * jax.nn.sigmoid(a[:, d:])
    ubuf[:, CONV_PAD:CONV_PAD + tt, :] = u.reshape(bb, tt, d)

    nrb = tt // CONV_ROW_BLOCK
    lead = CONV_PAD - CONV_HIST

    def row_block(i, carry):
        b = i // nrb
        r0 = pl.multiple_of((i % nrb) * CONV_ROW_BLOCK, CONV_ROW_BLOCK)
        for c in range(d // LANES):
            cs = slice(c * LANES, (c + 1) * LANES)
            win = ubuf[b, pl.ds(r0, CONV_ROW_BLOCK + CONV_PAD), cs]
            wts = dw_ref[:, cs]
            acc = jnp.zeros((CONV_ROW_BLOCK, LANES), F32)
            for k in range(CONV_WIDTH):
                acc = acc + win[lead + k:lead + k + CONV_ROW_BLOCK, :] * wts[k:k + 1, :]
            cbuf[b, pl.ds(r0, CONV_ROW_BLOCK), cs] = acc
        return carry

    lax.fori_loop(0, bb * nrb, row_block, 0)

    c = cbuf[...].reshape(bb * tt, d) + dwb_ref[...]
    mu = jnp.mean(c, axis=-1, keepdims=True)
    var = jnp.mean(jnp.square(c - mu), axis=-1, keepdims=True)
    c = (c - mu) * lax.rsqrt(var + LN_EPS) * lng_ref[...] + lnb_ref[...]
    c = c * jax.nn.sigmoid(c)
    y = _mm(c, w2_ref[...]) + b2_ref[...]
    y_ref[...] = (x + y).reshape(bb, tt, d)

    new_hist = ubuf[:, tt:tt + CONV_PAD, :]
    nh_ref[...] = new_hist
    ubuf[:, 0:CONV_PAD, :] = new_hist


def _conv_layer(x, hist, g, w1, b1, dw, dwb, lng, lnb, w2, b2, *, bb, tt):
    B, T, d = x.shape
    assert B % bb == 0 and T % tt == 0 and tt % CONV_ROW_BLOCK == 0 and tt >= CONV_PAD
    hist_p = jnp.pad(hist, ((0, 0), (CONV_PAD - CONV_HIST, 0), (0, 0)))
    dw_p = jnp.pad(dw, ((0, CONV_PAD - CONV_WIDTH), (0, 0)))
    row = lambda v: v.reshape(1, -1)
    y, nh = pl.pallas_call(
        functools.partial(_conv_kernel, bb=bb, tt=tt),
        grid=(B // bb, T // tt),
        in_specs=[
            pl.BlockSpec((bb, tt, d), lambda b, t: (b, t, 0)),
            pl.BlockSpec((bb, CONV_PAD, d), lambda b, t: (b, 0, 0)),
            _full((1, d)), _full((d, 2 * d)), _full((1, 2 * d)), _full((CONV_PAD, d)), _full((1, d)),
            _full((1, d)), _full((1, d)), _full((d, d)), _full((1, d)),
        ],
        out_specs=[
            pl.BlockSpec((bb, tt, d), lambda b, t: (b, t, 0)),
            pl.BlockSpec((bb, CONV_PAD, d), lambda b, t: (b, 0, 0)),
        ],
        out_shape=[jax.ShapeDtypeStruct((B, T, d), F32), jax.ShapeDtypeStruct((B, CONV_PAD, d), F32)],
        scratch_shapes=[pltpu.VMEM((bb, CONV_PAD + tt, d), F32), pltpu.VMEM((bb, tt, d), F32)],
        compiler_params=pltpu.CompilerParams(dimension_semantics=("arbitrary", "arbitrary"),
                                             vmem_limit_bytes=VMEM_LIMIT),
        name="conv_mixer",
    )(x, hist_p, row(g), w1.astype(MXU_DT), row(b1), dw_p, row(dwb), row(lng), row(lnb),
      w2.astype(MXU_DT), row(b2))
    return y, nh[:, CONV_PAD - CONV_HIST:, :]


MLSTM_ROW_BLOCK = 32


def _mlstm_kernel(x_ref, hist_ref, c0_ref, n0_ref, m0_ref, g_ref, wup_ref, cw_ref, cb_ref, wq_ref, wk_ref,
                  wv_ref, wg_ref, bg_ref, ng_ref, sk_ref, wd_ref,
                  y_ref, nh_ref, c_ref, n_ref, m_ref, cbuf, xc_s, qkv_s, hh_s, *, L):
    d, di, nh, dh = D_MODEL, MLSTM_INNER, MLSTM_HEADS, MLSTM_HEAD_DIM
    t = pl.program_id(1)

    @pl.when(t == 0)
    def _():
        cbuf[0:MLSTM_CONV_PAD, :] = hist_ref[0]
        c_ref[...] = c0_ref[...]
        n_ref[...] = n0_ref[...]
        m_ref[...] = m0_ref[...]

    x = x_ref[0]
    h = _rms(x, g_ref[...])
    up = _mm(h, wup_ref[...])
    xm = up[:, :di]
    z = up[:, di:]
    cbuf[MLSTM_CONV_PAD:MLSTM_CONV_PAD + L, :] = xm

    lead = MLSTM_CONV_PAD - MLSTM_CONV_HIST

    def row_block(i, carry):
        r0 = pl.multiple_of(i * MLSTM_ROW_BLOCK, MLSTM_ROW_BLOCK)
        for c in range(di // LANES):
            cs = slice(c * LANES, (c + 1) * LANES)
            win = cbuf[pl.ds(r0, MLSTM_ROW_BLOCK + MLSTM_CONV_PAD), cs]
            wts = cw_ref[:, cs]
            acc = jnp.zeros((MLSTM_ROW_BLOCK, LANES), F32) + cb_ref[:, cs]
            for k in range(MLSTM_CONV_WIDTH):
                acc = acc + win[lead + k:lead + k + MLSTM_ROW_BLOCK, :] * wts[k:k + 1, :]
            xc_s[pl.ds(r0, MLSTM_ROW_BLOCK), cs] = acc * jax.nn.sigmoid(acc)
        return carry

    lax.fori_loop(0, L // MLSTM_ROW_BLOCK, row_block, 0)
    new_hist = cbuf[L:L + MLSTM_CONV_PAD, :]
    nh_ref[0] = new_hist
    cbuf[0:MLSTM_CONV_PAD, :] = new_hist

    xc = xc_s[...]
    for c in range(di // MXU_TILE):
        cs = slice(c * MXU_TILE, (c + 1) * MXU_TILE)
        qkv_s[:, c * MXU_TILE:(c + 1) * MXU_TILE] = _mm(xc[:, cs], wq_ref[c])
        qkv_s[:, di + c * MXU_TILE:di + (c + 1) * MXU_TILE] = _mm(xc[:, cs], wk_ref[c])
        qkv_s[:, 2 * di + c * MXU_TILE:2 * di + (c + 1) * MXU_TILE] = _mm(xm[:, cs], wv_ref[c])

    gates = _mm_nt(wg_ref[...], qkv_s[...]) + bg_ref[...]
    ig_all = gates[0:nh, :]
    fp = gates[0:SUBLANES, :]
    lf_all = jnp.minimum(fp, 0.0) - jnp.log(1.0 + jnp.exp(-jnp.abs(fp)))
    ri = lax.broadcasted_iota(jnp.int32, (L, L), 0)
    ci = lax.broadcasted_iota(jnp.int32, (L, L), 1)
    causal = ri >= ci
    eye = ri == ci
    tri = jnp.where(ri <= ci, 1.0, 0.0).astype(F32)
    b_all = jnp.dot(lf_all, tri, preferred_element_type=F32, precision=lax.Precision.HIGHEST)

    def to_col(row):
        return jnp.sum(jnp.where(eye, row, 0.0), axis=1, keepdims=True)

    kscale = dh ** -0.5
    for hd in range(nh):
        hs = slice(hd * dh, (hd + 1) * dh)
        qh = qkv_s[:, hd * dh:(hd + 1) * dh]
        kh = qkv_s[:, di + hd * dh:di + (hd + 1) * dh] * kscale
        vh = qkv_s[:, 2 * di + hd * dh:2 * di + (hd + 1) * dh]
        ig_r = ig_all[hd:hd + 1, :]
        b_r = b_all[nh + hd:nh + hd + 1, :]
        b_c = to_col(b_r)
        m_prev = m_ref[0, hd:hd + 1, 0:1]
        n_r = n_ref[0, hd:hd + 1, :]
        cmat = c_ref[0, hd]

        logw = jnp.where(causal, b_c - b_r + ig_r, NEG_INF)
        inter = b_c + m_prev
        m_row = jnp.maximum(jnp.max(logw, axis=1, keepdims=True), inter)
        w_intra = jnp.exp(logw - m_row)
        w_inter = jnp.exp(inter - m_row)
        s = _mm_nt(qh, kh) * w_intra
        num = _mm(s, vh) + w_inter * _mm(qh, cmat)
        den = jnp.sum(s, axis=1, keepdims=True) + w_inter * jnp.sum(qh * n_r, axis=1, keepdims=True)
        hh = num / jnp.maximum(jnp.abs(den), jnp.exp(-m_row))

        mu = jnp.mean(hh, axis=-1, keepdims=True)
        var = jnp.mean(jnp.square(hh - mu), axis=-1, keepdims=True)
        hh_s[:, hs] = (hh - mu) * lax.rsqrt(var + LN_EPS)

        b_end = b_r[:, L - 1:L]
        g_r = b_end - b_r + ig_r
        m_new = jnp.maximum(b_end + m_prev, jnp.max(g_r, axis=1, keepdims=True))
        wk_c = to_col(jnp.exp(g_r - m_new))
        decay = jnp.exp(b_end + m_prev - m_new)
        kw = kh * wk_c
        upd = lax.dot_general(kw.astype(MXU_DT), vh.astype(MXU_DT), (((0,), (0,)), ((), ())),
                              preferred_element_type=F32)
        c_ref[0, hd] = decay * cmat + upd
        n_ref[0, hd:hd + 1, :] = decay * n_r + jnp.sum(kw, axis=0, keepdims=True)
        m_ref[0, hd:hd + 1, :] = jnp.broadcast_to(m_new, (1, LANES))

    hn = hh_s[...] * ng_ref[...]
    out = jax.nn.sigmoid(z) * (hn + sk_ref[...] * xc)
    y_ref[0] = x + _mm(out, wd_ref[...])


def _blockdiag_tiles(w):
    g, bs, _ = w.shape
    per = MXU_TILE // bs
    wt = w.reshape(g // per, per, bs, bs)
    eye = jnp.eye(per, dtype=w.dtype)
    dense = jnp.einsum("tpij,pq->tpiqj", wt, eye)
    return dense.reshape(g // per, MXU_TILE, MXU_TILE)


def _mlstm_layer(x, hist, c0, n0, m0, g, wup, cw, cb, wq, wk, wv, wi, bi, wf, bf, ng, sk, wd, *, chunk):
    B, T, d = x.shape
    di, nh, dh, L = MLSTM_INNER, MLSTM_HEADS, MLSTM_HEAD_DIM, chunk
    assert T % L == 0 and L % MLSTM_ROW_BLOCK == 0
    hist_p = jnp.pad(hist, ((0, 0), (MLSTM_CONV_PAD - MLSTM_CONV_HIST, 0), (0, 0)))
    cw_p = jnp.pad(cw, ((0, SUBLANES - MLSTM_CONV_WIDTH), (0, 0)))
    n0_p = jnp.pad(n0, ((0, 0), (0, SUBLANES - nh), (0, 0)))
    m0_p = jnp.broadcast_to(jnp.pad(m0, ((0, 0), (0, SUBLANES - nh)))[:, :, None], (B, SUBLANES, LANES))
    wg = jnp.pad(jnp.concatenate([wi, wf], axis=1).T, ((0, GATE_ROWS - 2 * nh), (0, 0)))
    bg = jnp.pad(jnp.concatenate([bi, bf]), (0, GATE_ROWS - 2 * nh)).reshape(GATE_ROWS, 1)
    row = lambda v: v.reshape(1, -1)
    nt = di // MXU_TILE
    y, nhist, c_new, n_new, m_new = pl.pallas_call(
        functools.partial(_mlstm_kernel, L=L),
        grid=(B, T // L),
        in_specs=[
            pl.BlockSpec((1, L, d), lambda b, t: (b, t, 0)),
            pl.BlockSpec((1, MLSTM_CONV_PAD, di), lambda b, t: (b, 0, 0)),
            pl.BlockSpec((1, nh, dh, dh), lambda b, t: (b, 0, 0, 0)),
            pl.BlockSpec((1, SUBLANES, dh), lambda b, t: (b, 0, 0)),
            pl.BlockSpec((1, SUBLANES, LANES), lambda b, t: (b, 0, 0)),
            _full((1, d)), _full((d, 2 * di)), _full((SUBLANES, di)), _full((1, di)),
            _full((nt, MXU_TILE, MXU_TILE)), _full((nt, MXU_TILE, MXU_TILE)), _full((nt, MXU_TILE, MXU_TILE)),
            _full((GATE_ROWS, 3 * di)), _full((GATE_ROWS, 1)), _full((1, di)), _full((1, di)), _full((di, d)),
        ],
        out_specs=[
            pl.BlockSpec((1, L, d), lambda b, t: (b, t, 0)),
            pl.BlockSpec((1, MLSTM_CONV_PAD, di), lambda b, t: (b, 0, 0)),
            pl.BlockSpec((1, nh, dh, dh), lambda b, t: (b, 0, 0, 0)),
            pl.BlockSpec((1, SUBLANES, dh), lambda b, t: (b, 0, 0)),
            pl.BlockSpec((1, SUBLANES, LANES), lambda b, t: (b, 0, 0)),
        ],
        out_shape=[
            jax.ShapeDtypeStruct((B, T, d), F32),
            jax.ShapeDtypeStruct((B, MLSTM_CONV_PAD, di), F32),
            jax.ShapeDtypeStruct((B, nh, dh, dh), F32),
            jax.ShapeDtypeStruct((B, SUBLANES, dh), F32),
            jax.ShapeDtypeStruct((B, SUBLANES, LANES), F32),
        ],
        scratch_shapes=[
            pltpu.VMEM((MLSTM_CONV_PAD + L, di), F32),
            pltpu.VMEM((L, di), F32),
            pltpu.VMEM((L, 3 * di), F32),
            pltpu.VMEM((L, di), F32),
        ],
        compiler_params=pltpu.CompilerParams(dimension_semantics=("arbitrary", "arbitrary"),
                                             vmem_limit_bytes=VMEM_LIMIT),
        name="mlstm_mixer",
    )(x, hist_p, c0, n0_p, m0_p, row(g), wup.astype(MXU_DT), cw_p, row(cb),
      _blockdiag_tiles(wq).astype(MXU_DT), _blockdiag_tiles(wk).astype(MXU_DT),
      _blockdiag_tiles(wv).astype(MXU_DT), wg.astype(MXU_DT), bg, row(ng), row(sk), wd.astype(MXU_DT))
    return (y, nhist[:, MLSTM_CONV_PAD - MLSTM_CONV_HIST:, :], c_new, n_new[:, :nh, :], m_new[:, :nh, 0])


def _top16(tile):
    vals = []
    cur = tile
    for a in range(PEER_TOPK):
        mx = jnp.max(cur, axis=0, keepdims=True)
        vals.append(mx)
        if a + 1 < PEER_TOPK:
            cur = jnp.where(cur == mx, NEG_INF, cur)
    return vals


def _pack_heads(rows):
    sub = lax.broadcasted_iota(jnp.int32, (PEER_HEADS, LANES), 0)
    out = jnp.broadcast_to(rows[0], (PEER_HEADS, LANES))
    for hd in range(1, PEER_HEADS):
        out = jnp.where(sub == hd, rows[hd], out)
    return out


def _route_kernel(x_ref, g_ref, wqt_ref, k1_ref, k2_ref,
                  xt_ref, thr_ref, p1_ref, s2_ref, p2_ref, s1_s, *, tt):
    ng = tt // LANES
    xn = _rms(x_ref[...], g_ref[...])
    xt = xn.T.astype(MXU_DT)
    xt_ref[...] = xt
    qt = jnp.dot(wqt_ref[...], xt, preferred_element_type=F32)
    for hd in range(PEER_HEADS):
        q1 = qt[hd * 2 * PEER_HALF:hd * 2 * PEER_HALF + PEER_HALF, :]
        q2 = qt[hd * 2 * PEER_HALF + PEER_HALF:(hd + 1) * 2 * PEER_HALF, :]
        s1 = _mm(k1_ref[hd], q1)
        s2 = _mm(k2_ref[hd], q2)
        for gi in range(ng):
            s1_s[gi, hd] = s1[:, gi * LANES:(gi + 1) * LANES]
            s2_ref[gi, hd] = s2[:, gi * LANES:(gi + 1) * LANES]

    pairs = [(a, b) for a in range(PEER_TOPK) for b in range(PEER_TOPK) if (a + 1) * (b + 1) <= PEER_TOPK]

    def group(gi, carry):
        v1 = [_top16(s1_s[gi, hd]) for hd in range(PEER_HEADS)]
        v2 = [_top16(s2_ref[gi, hd]) for hd in range(PEER_HEADS)]
        V1 = [_pack_heads([v1[hd][a] for hd in range(PEER_HEADS)]) for a in range(PEER_TOPK)]
        V2 = [_pack_heads([v2[hd][a] for hd in range(PEER_HEADS)]) for a in range(PEER_TOPK)]
        cand = [V1[a] + V2[b] for (a, b) in pairs]
        cur = list(cand)
        tau = None
        for it in range(PEER_TOPK):
            mx = cur[0]
            for cnd in cur[1:]:
                mx = jnp.maximum(mx, cnd)
            if it + 1 < PEER_TOPK:
                cur = [jnp.where(cnd == mx, NEG_INF, cnd) for cnd in cur]
            else:
                tau = mx
        E1 = [jnp.exp(V1[a] - V1[0]) for a in range(PEER_TOPK)]
        E2 = [jnp.exp(V2[b] - V2[0]) for b in range(PEER_TOPK)]
        zsum = jnp.zeros((PEER_HEADS, LANES), F32)
        for idx, (a, b) in enumerate(pairs):
            zsum = zsum + jnp.where(cand[idx] >= tau, E1[a] * E2[b], 0.0)
        inv_z = 1.0 / zsum
        for hd in range(PEER_HEADS):
            tau_h = tau[hd:hd + 1, :]
            s1 = s1_s[gi, hd]
            thr = jnp.full((PEER_N_KEYS, LANES), POS_INF, F32)
            for b in range(PEER_TOPK):
                vb = v2[hd][b]
                thr = jnp.minimum(thr, jnp.where(s1 + vb >= tau_h, vb, POS_INF))
            thr_ref[gi, hd] = thr
            p1_ref[gi, hd] = jnp.exp(s1 - v1[hd][0]) * inv_z[hd:hd + 1, :]
            p2_ref[gi, hd] = jnp.exp(s2_ref[gi, hd] - v2[hd][0])
        return carry

    lax.fori_loop(0, ng, group, 0)


def _peer_route(x, g, wqt, k1, k2, *, tt):
    n, d = x.shape
    assert n % tt == 0 and tt % LANES == 0
    ng = tt // LANES
    tile_spec = pl.BlockSpec((ng, PEER_HEADS, PEER_N_KEYS, LANES), lambda i: (i, 0, 0, 0))
    tile_shape = jax.ShapeDtypeStruct((n // LANES, PEER_HEADS, PEER_N_KEYS, LANES), F32)
    return pl.pallas_call(
        functools.partial(_route_kernel, tt=tt),
        grid=(n // tt,),
        in_specs=[
            pl.BlockSpec((tt, d), lambda i: (i, 0)),
            _full((1, d)), _full((2 * PEER_HALF * PEER_HEADS, d)),
            _full((PEER_HEADS, PEER_N_KEYS, PEER_HALF)), _full((PEER_HEADS, PEER_N_KEYS, PEER_HALF)),
        ],
        out_specs=[pl.BlockSpec((d, tt), lambda i: (0, i)), tile_spec, tile_spec, tile_spec, tile_spec],
        out_shape=[jax.ShapeDtypeStruct((d, n), MXU_DT), tile_shape, tile_shape, tile_shape, tile_shape],
        scratch_shapes=[pltpu.VMEM((ng, PEER_HEADS, PEER_N_KEYS, LANES), F32)],
        compiler_params=pltpu.CompilerParams(dimension_semantics=("arbitrary",),
                                             vmem_limit_bytes=VMEM_LIMIT),
        name="peer_route",
    )(x, g.reshape(1, d), wqt, k1, k2)


PEER_SUB = 512
PEER_WIDE = MXU_TILE


def _gelu_tanh(x):
    return 0.5 * x * (1.0 + jnp.tanh(0.7978845608028654 * (x + 0.044715 * (x * x * x))))


def _expert_kernel(x_ref, xt_ref, thr_ref, p1_ref, s2_ref, p2_ref, u_ref, vt_ref, y_ref,
                   acc_s, s_s, w_s, *, tt, ec):
    c = pl.program_id(1)
    ngw = tt // PEER_WIDE
    halves = PEER_WIDE // LANES
    jn = PEER_SUB // PEER_N_KEYS

    @pl.when(c == 0)
    def _():
        acc_s[...] = jnp.zeros_like(acc_s)

    for sub in range(ec // PEER_SUB):
        u_sub = u_ref[sub * PEER_SUB:(sub + 1) * PEER_SUB, :]
        for gw in range(ngw):
            s_s[gw] = jnp.dot(u_sub, xt_ref[:, gw * PEER_WIDE:(gw + 1) * PEER_WIDE],
                              preferred_element_type=F32)

        def tile(i, carry):
            jj = i // ngw
            gw = i % ngw
            j = sub * jn + jj
            r0 = pl.multiple_of(jj * PEER_N_KEYS, PEER_N_KEYS)
            for half in range(halves):
                gi = gw * halves + half
                ls = slice(half * LANES, (half + 1) * LANES)
                gate = jnp.zeros((PEER_N_KEYS, LANES), F32)
                for hd in range(PEER_HEADS):
                    thr = thr_ref[gi, hd, pl.ds(j, 1), :]
                    p1 = p1_ref[gi, hd, pl.ds(j, 1), :]
                    gate = gate + jnp.where(s2_ref[gi, hd] >= thr, p2_ref[gi, hd], 0.0) * p1
                act = _gelu_tanh(s_s[gw, pl.ds(r0, PEER_N_KEYS), ls])
                w_s[gw, pl.ds(r0, PEER_N_KEYS), ls] = (gate * act).astype(MXU_DT)
            return carry

        lax.fori_loop(0, jn * ngw, tile, 0)
        v_sub = vt_ref[:, sub * PEER_SUB:(sub + 1) * PEER_SUB]
        for gw in range(ngw):
            acc_s[:, gw * PEER_WIDE:(gw + 1) * PEER_WIDE] += jnp.dot(v_sub, w_s[gw],
                                                                     preferred_element_type=F32)

    @pl.when(c == pl.num_programs(1) - 1)
    def _():
        y_ref[...] = x_ref[...] + acc_s[...].T


def _peer_experts(x, xt, thr, p1, s2, p2, u, vt, *, tt, ec):
    n, d = x.shape
    ne = u.shape[0]
    assert n % tt == 0 and tt % PEER_WIDE == 0 and ne % ec == 0 and ec % PEER_SUB == 0
    ng = tt // LANES
    jc = ec // PEER_N_KEYS
    key_chunk = pl.BlockSpec((ng, PEER_HEADS, jc, LANES), lambda i, c: (i, 0, c, 0))
    all_keys = pl.BlockSpec((ng, PEER_HEADS, PEER_N_KEYS, LANES), lambda i, c: (i, 0, 0, 0))
    return pl.pallas_call(
        functools.partial(_expert_kernel, tt=tt, ec=ec),
        grid=(n // tt, ne // ec),
        in_specs=[
            pl.BlockSpec((tt, d), lambda i, c: (i, 0)),
            pl.BlockSpec((d, tt), lambda i, c: (0, i)),
            key_chunk, key_chunk, all_keys, all_keys,
            pl.BlockSpec((ec, d), lambda i, c: (c, 0)),
            pl.BlockSpec((d, ec), lambda i, c: (0, c)),
        ],
        out_specs=pl.BlockSpec((tt, d), lambda i, c: (i, 0)),
        out_shape=jax.ShapeDtypeStruct((n, d), F32),
        scratch_shapes=[pltpu.VMEM((d, tt), F32), pltpu.VMEM((tt // PEER_WIDE, PEER_SUB, PEER_WIDE), F32),
                        pltpu.VMEM((tt // PEER_WIDE, PEER_SUB, PEER_WIDE), MXU_DT)],
        compiler_params=pltpu.CompilerParams(dimension_semantics=("arbitrary", "arbitrary"),
                                             vmem_limit_bytes=VMEM_LIMIT),
        name="peer_experts",
    )(x, xt, thr, p1, s2, p2, u, vt)


def _peer_layer(x, g, wq, k1, k2, u, v, *, tt_route, tt, ec):
    xt, thr, p1, s2, p2 = _peer_route(x, g, wq.T.astype(MXU_DT), k1.astype(MXU_DT), k2.astype(MXU_DT),
                                      tt=tt_route)
    return _peer_experts(x, xt, thr, p1, s2, p2, u.astype(MXU_DT), v.T.astype(MXU_DT), tt=tt, ec=ec)


def _norm_kernel(x_ref, g_ref, y_ref):
    y_ref[...] = _rms(x_ref[...], g_ref[...])


def _final_norm(x, g, *, tt):
    n, d = x.shape
    return pl.pallas_call(
        _norm_kernel,
        grid=(n // tt,),
        in_specs=[pl.BlockSpec((tt, d), lambda i: (i, 0)), _full((1, d))],
        out_specs=pl.BlockSpec((tt, d), lambda i: (i, 0)),
        out_shape=jax.ShapeDtypeStruct((n, d), F32),
        name="final_norm",
    )(x, g.reshape(1, d))


CONV_PROMPT_TILE = 512
CONV_SAMPLE_SEQS = 8
PEER_ROUTE_TILE = 512
PEER_TOKEN_TILE = 512
PEER_EXPERT_CHUNK = 1024
NORM_TILE = 1024


def kernel(x_prompt, x_sample, cache_conv, state_mlstm_conv, state_C, state_n, state_m, norm_mix, norm_ffn, norm_final, cm_w1, cm_b1, cm_dw, cm_dwb, cm_ln_g, cm_ln_b, cm_w2, cm_b2, ml_wup, ml_convw, ml_convb, ml_wq, ml_wk, ml_wv, ml_wi, ml_bi, ml_wf, ml_bf, ml_norm, ml_skip, ml_wdown, pk_wq, pk_k1, pk_k2, pk_u, pk_v):
    xp, xs = x_prompt, x_sample
    bp, tp, d = xp.shape
    bs, ts, _ = xs.shape
    di, nh, dh = MLSTM_INNER, MLSTM_HEADS, MLSTM_HEAD_DIM
    conv_p, conv_s, mconv_p, mconv_s = [], [], [], []
    cp_l, cs_l, np_l, ns_l, mp_l, ms_l = [], [], [], [], [], []
    for layer in range(DEPTH):
        j = layer // N_MIXERS
        if layer % N_MIXERS == 0:
            cw = (norm_mix[layer], cm_w1[j], cm_b1[j], cm_dw[j], cm_dwb[j], cm_ln_g[j], cm_ln_b[j],
                  cm_w2[j], cm_b2[j])
            xp, hist_p = _conv_layer(xp, jnp.zeros((bp, CONV_HIST, d), F32), *cw, bb=1,
                                     tt=min(tp, CONV_PROMPT_TILE))
            xs, hist_s = _conv_layer(xs, cache_conv[j], *cw, bb=min(bs, CONV_SAMPLE_SEQS), tt=ts)
            conv_p.append(hist_p)
            conv_s.append(hist_s)
        else:
            mw = (norm_mix[layer], ml_wup[j], ml_convw[j], ml_convb[j], ml_wq[j], ml_wk[j], ml_wv[j],
                  ml_wi[j], ml_bi[j], ml_wf[j], ml_bf[j], ml_norm[j], ml_skip[j], ml_wdown[j])
            xp, hcp, c_p, n_p, m_p = _mlstm_layer(
                xp, jnp.zeros((bp, MLSTM_CONV_HIST, di), F32), jnp.zeros((bp, nh, dh, dh), F32),
                jnp.zeros((bp, nh, dh), F32), jnp.zeros((bp, nh), F32), *mw,
                chunk=min(tp, MLSTM_PROMPT_CHUNK))
            xs, hcs, c_s, n_s, m_s = _mlstm_layer(
                xs, state_mlstm_conv[j], state_C[j], state_n[j], state_m[j], *mw, chunk=ts)
            mconv_p.append(hcp)
            mconv_s.append(hcs)
            cp_l.append(c_p)
            cs_l.append(c_s)
            np_l.append(n_p)
            ns_l.append(n_s)
            mp_l.append(m_p)
            ms_l.append(m_s)
        flat = jnp.concatenate([xp.reshape(bp * tp, d), xs.reshape(bs * ts, d)], axis=0)
        flat = _peer_layer(flat, norm_ffn[layer], pk_wq[layer], pk_k1[layer], pk_k2[layer], pk_u[layer],
                           pk_v[layer], tt_route=PEER_ROUTE_TILE, tt=PEER_TOKEN_TILE, ec=PEER_EXPERT_CHUNK)
        if layer == DEPTH - 1:
            flat = _final_norm(flat, norm_final, tt=NORM_TILE)
        xp = flat[:bp * tp].reshape(bp, tp, d)
        xs = flat[bp * tp:].reshape(bs, ts, d)
    return (xp, xs,
            jnp.stack(conv_p), jnp.stack(conv_s),
            jnp.stack(mconv_p), jnp.stack(mconv_s),
            jnp.stack(cp_l), jnp.stack(cs_l),
            jnp.stack(np_l), jnp.stack(ns_l),
            jnp.stack(mp_l), jnp.stack(ms_l))
```

```python
import functools

import jax
import jax.numpy as jnp
from jax import lax
from jax.experimental import pallas as pl
from jax.experimental.pallas import tpu as pltpu

F32 = jnp.float32
MXU_DT = jnp.bfloat16

D_MODEL = 1024
DEPTH = 4
N_MIXERS = 2
CONV_WIDTH = 31
CONV_HIST = CONV_WIDTH - 1
CONV_PAD = 32
MLSTM_INNER = 2 * D_MODEL
MLSTM_HEADS = 4
MLSTM_HEAD_DIM = MLSTM_INNER // MLSTM_HEADS
MLSTM_QKV_BLOCK = 4
MLSTM_CONV_WIDTH = 4
MLSTM_CONV_HIST = MLSTM_CONV_WIDTH - 1
MLSTM_CONV_PAD = 8
MLSTM_PROMPT_CHUNK = 256
PEER_HEADS = 8
PEER_N_KEYS = 128
PEER_N_EXPERTS = PEER_N_KEYS * PEER_N_KEYS
PEER_TOPK = 16
PEER_HALF = 128
RMS_EPS = 1e-6
LN_EPS = 1e-5

LANES = 128
SUBLANES = 8
MXU_TILE = 256
GATE_ROWS = 16
VMEM_LIMIT = 56 * 1024 * 1024
NEG_INF = float("-inf")
POS_INF = float("inf")


def _full(shape):
    n = len(shape)
    return pl.BlockSpec(shape, lambda *_: (0,) * n)


def _rms(x, g):
    return x * lax.rsqrt(jnp.mean(x * x, axis=-1, keepdims=True) + RMS_EPS) * g


def _mm(a, b):
    return jnp.dot(a.astype(MXU_DT), b.astype(MXU_DT), preferred_element_type=F32)


def _mm_nt(a, b):
    return lax.dot_general(a.astype(MXU_DT), b.astype(MXU_DT), (((1,), (1,)), ((), ())),
                           preferred_element_type=F32)


CONV_ROW_BLOCK = 32


def _conv_kernel(x_ref, hist_ref, g_ref, w1_ref, b1_ref, dw_ref, dwb_ref, lng_ref, lnb_ref, w2_ref, b2_ref,
                 y_ref, nh_ref, ubuf, cbuf, *, bb, tt):
    d = D_MODEL
    t = pl.program_id(1)

    @pl.when(t == 0)
    def _():
        ubuf[:, 0:CONV_PAD, :] = hist_ref[...]

    x = x_ref[...].reshape(bb * tt, d)
    h = _rms(x, g_ref[...])
    a = _mm(h, w1_ref[...]) + b1_ref[...]
    u = a[:, :d] * jax.nn.sigmoid(a[:, d:])
    ubuf[:, CONV_PAD:CONV_PAD + tt, :] = u.reshape(bb, tt, d)

    nrb = tt // CONV_ROW_BLOCK
    lead = CONV_PAD - CONV_HIST

    def row_block(i, carry):
        b = i // nrb
        r0 = pl.multiple_of((i % nrb) * CONV_ROW_BLOCK, CONV_ROW_BLOCK)
        for c in range(d // LANES):
            cs = slice(c * LANES, (c + 1) * LANES)
            win = ubuf[b, pl.ds(r0, CONV_ROW_BLOCK + CONV_PAD), cs]
            wts = dw_ref[:, cs]
            acc = jnp.zeros((CONV_ROW_BLOCK, LANES), F32)
            for k in range(CONV_WIDTH):
                acc = acc + win[lead + k:lead + k + CONV_ROW_BLOCK, :] * wts[k:k + 1, :]
            cbuf[b, pl.ds(r0, CONV_ROW_BLOCK), cs] = acc
        return carry

    lax.fori_loop(0, bb * nrb, row_block, 0)

    c = cbuf[...].reshape(bb * tt, d) + dwb_ref[...]
    mu = jnp.mean(c, axis=-1, keepdims=True)
    var = jnp.mean(jnp.square(c - mu), axis=-1, keepdims=True)
    c = (c - mu) * lax.rsqrt(var + LN_EPS) * lng_ref[...] + lnb_ref[...]
    c = c * jax.nn.sigmoid(c)
    y = _mm(c, w2_ref[...]) + b2_ref[...]
    y_ref[...] = (x + y).reshape(bb, tt, d)

    new_hist = ubuf[:, tt:tt + CONV_PAD, :]
    nh_ref[...] = new_hist
    ubuf[:, 0:CONV_PAD, :] = new_hist


def _conv_layer(x, hist, g, w1, b1, dw, dwb, lng, lnb, w2, b2, *, bb, tt):
    B, T, d = x.shape
    assert B % bb == 0 and T % tt == 0 and tt % CONV_ROW_BLOCK == 0 and tt >= CONV_PAD
    hist_p = jnp.pad(hist, ((0, 0), (CONV_PAD - CONV_HIST, 0), (0, 0)))
    dw_p = jnp.pad(dw, ((0, CONV_PAD - CONV_WIDTH), (0, 0)))
    row = lambda v: v.reshape(1, -1)
    y, nh = pl.pallas_call(
        functools.partial(_conv_kernel, bb=bb, tt=tt),
        grid=(B // bb, T // tt),
        in_specs=[
            pl.BlockSpec((bb, tt, d), lambda b, t: (b, t, 0)),
            pl.BlockSpec((bb, CONV_PAD, d), lambda b, t: (b, 0, 0)),
            _full((1, d)), _full((d, 2 * d)), _full((1, 2 * d)), _full((CONV_PAD, d)), _full((1, d)),
            _full((1, d)), _full((1, d)), _full((d, d)), _full((1, d)),
        ],
        out_specs=[
            pl.BlockSpec((bb, tt, d), lambda b, t: (b, t, 0)),
            pl.BlockSpec((bb, CONV_PAD, d), lambda b, t: (b, 0, 0)),
        ],
        out_shape=[jax.ShapeDtypeStruct((B, T, d), F32), jax.ShapeDtypeStruct((B, CONV_PAD, d), F32)],
        scratch_shapes=[pltpu.VMEM((bb, CONV_PAD + tt, d), F32), pltpu.VMEM((bb, tt, d), F32)],
        compiler_params=pltpu.CompilerParams(dimension_semantics=("arbitrary", "arbitrary"),
                                             vmem_limit_bytes=VMEM_LIMIT),
        name="conv_mixer",
    )(x, hist_p, row(g), w1.astype(MXU_DT), row(b1), dw_p, row(dwb), row(lng), row(lnb),
      w2.astype(MXU_DT), row(b2))
    return y, nh[:, CONV_PAD - CONV_HIST:, :]


MLSTM_ROW_BLOCK = 32


def _mlstm_kernel(x_ref, hist_ref, c0_ref, n0_ref, m0_ref, g_ref, wup_ref, cw_ref, cb_ref, wq_ref, wk_ref,
                  wv_ref, wg_ref, bg_ref, ng_ref, sk_ref, wd_ref,
                  y_ref, nh_ref, c_ref, n_ref, m_ref, cbuf, xc_s, qkv_s, hh_s, *, L):
    d, di, nh, dh = D_MODEL, MLSTM_INNER, MLSTM_HEADS, MLSTM_HEAD_DIM
    t = pl.program_id(1)

    @pl.when(t == 0)
    def _():
        cbuf[0:MLSTM_CONV_PAD, :] = hist_ref[0]
        c_ref[...] = c0_ref[...]
        n_ref[...] = n0_ref[...]
        m_ref[...] = m0_ref[...]

    x = x_ref[0]
    h = _rms(x, g_ref[...])
    up = _mm(h, wup_ref[...])
    xm = up[:, :di]
    z = up[:, di:]
    cbuf[MLSTM_CONV_PAD:MLSTM_CONV_PAD + L, :] = xm

    lead = MLSTM_CONV_PAD - MLSTM_CONV_HIST

    def row_block(i, carry):
        r0 = pl.multiple_of(i * MLSTM_ROW_BLOCK, MLSTM_ROW_BLOCK)
        for c in range(di // LANES):
            cs = slice(c * LANES, (c + 1) * LANES)
            win = cbuf[pl.ds(r0, MLSTM_ROW_BLOCK + MLSTM_CONV_PAD), cs]
            wts = cw_ref[:, cs]
            acc = jnp.zeros((MLSTM_ROW_BLOCK, LANES), F32) + cb_ref[:, cs]
            for k in range(MLSTM_CONV_WIDTH):
                acc = acc + win[lead + k:lead + k + MLSTM_ROW_BLOCK, :] * wts[k:k + 1, :]
            xc_s[pl.ds(r0, MLSTM_ROW_BLOCK), cs] = acc * jax.nn.sigmoid(acc)
        return carry

    lax.fori_loop(0, L // MLSTM_ROW_BLOCK, row_block, 0)
    new_hist = cbuf[L:L + MLSTM_CONV_PAD, :]
    nh_ref[0] = new_hist
    cbuf[0:MLSTM_CONV_PAD, :] = new_hist

    xc = xc_s[...]
    for c in range(di // MXU_TILE):
        cs = slice(c * MXU_TILE, (c + 1) * MXU_TILE)
        qkv_s[:, c * MXU_TILE:(c + 1) * MXU_TILE] = _mm(xc[:, cs], wq_ref[c])
        qkv_s[:, di + c * MXU_TILE:di + (c + 1) * MXU_TILE] = _mm(xc[:, cs], wk_ref[c])
        qkv_s[:, 2 * di + c * MXU_TILE:2 * di + (c + 1) * MXU_TILE] = _mm(xm[:, cs], wv_ref[c])

    gates = _mm_nt(wg_ref[...], qkv_s[...]) + bg_ref[...]
    ig_all = gates[0:nh, :]
    fp = gates[0:SUBLANES, :]
    lf_all = jnp.minimum(fp, 0.0) - jnp.log(1.0 + jnp.exp(-jnp.abs(fp)))
    ri = lax.broadcasted_iota(jnp.int32, (L, L), 0)
    ci = lax.broadcasted_iota(jnp.int32, (L, L), 1)
    causal = ri >= ci
    eye = ri == ci
    tri = jnp.where(ri <= ci, 1.0, 0.0).astype(F32)
    b_all = jnp.dot(lf_all, tri, preferred_element_type=F32, precision=lax.Precision.HIGHEST)

    def to_col(row):
        return jnp.sum(jnp.where(eye, row, 0.0), axis=1, keepdims=True)

    kscale = dh ** -0.5
    for hd in range(nh):
        hs = slice(hd * dh, (hd + 1) * dh)
        qh = qkv_s[:, hd * dh:(hd + 1) * dh]
        kh = qkv_s[:, di + hd * dh:di + (hd + 1) * dh] * kscale
        vh = qkv_s[:, 2 * di + hd * dh:2 * di + (hd + 1) * dh]
        ig_r = ig_all[hd:hd + 1, :]
        b_r = b_all[nh + hd:nh + hd + 1, :]
        b_c = to_col(b_r)
        m_prev = m_ref[0, hd:hd + 1, 0:1]
        n_r = n_ref[0, hd:hd + 1, :]
        cmat = c_ref[0, hd]

        logw = jnp.where(causal, b_c - b_r + ig_r, NEG_INF)
        inter = b_c + m_prev
        m_row = jnp.maximum(jnp.max(logw, axis=1, keepdims=True), inter)
        w_intra = jnp.exp(logw - m_row)
        w_inter = jnp.exp(inter - m_row)
        s = _mm_nt(qh, kh) * w_intra
        num = _mm(s, vh) + w_inter * _mm(qh, cmat)
        den = jnp.sum(s, axis=1, keepdims=True) + w_inter * jnp.sum(qh * n_r, axis=1, keepdims=True)
        hh = num / jnp.maximum(jnp.abs(den), jnp.exp(-m_row))

        mu = jnp.mean(hh, axis=-1, keepdims=True)
        var = jnp.mean(jnp.square(hh - mu), axis=-1, keepdims=True)
        hh_s[:, hs] = (hh - mu) * lax.rsqrt(var + LN_EPS)

        b_end = b_r[:, L - 1:L]
        g_r = b_end - b_r + ig_r
        m_new = jnp.maximum(b_end + m_prev, jnp.max(g_r, axis=1, keepdims=True))
        wk_c = to_col(jnp.exp(g_r - m_new))
        decay = jnp.exp(b_end + m_prev - m_new)
        kw = kh * wk_c
        upd = lax.dot_general(kw.astype(MXU_DT), vh.astype(MXU_DT), (((0,), (0,)), ((), ())),
                              preferred_element_type=F32)
        c_ref[0, hd] = decay * cmat + upd
        n_ref[0, hd:hd + 1, :] = decay * n_r + jnp.sum(kw, axis=0, keepdims=True)
        m_ref[0, hd:hd + 1, :] = jnp.broadcast_to(m_new, (1, LANES))

    hn = hh_s[...] * ng_ref[...]
    out = jax.nn.sigmoid(z) * (hn + sk_ref[...] * xc)
    y_ref[0] = x + _mm(out, wd_ref[...])


def _blockdiag_tiles(w):
    g, bs, _ = w.shape
    per = MXU_TILE // bs
    wt = w.reshape(g // per, per, bs, bs)
    eye = jnp.eye(per, dtype=w.dtype)
    dense = jnp.einsum("tpij,pq->tpiqj", wt, eye)
    return dense.reshape(g // per, MXU_TILE, MXU_TILE)


def _mlstm_layer(x, hist, c0, n0, m0, g, wup, cw, cb, wq, wk, wv, wi, bi, wf, bf, ng, sk, wd, *, chunk):
    B, T, d = x.shape
    di, nh, dh, L = MLSTM_INNER, MLSTM_HEADS, MLSTM_HEAD_DIM, chunk
    assert T % L == 0 and L % MLSTM_ROW_BLOCK == 0
    hist_p = jnp.pad(hist, ((0, 0), (MLSTM_CONV_PAD - MLSTM_CONV_HIST, 0), (0, 0)))
    cw_p = jnp.pad(cw, ((0, SUBLANES - MLSTM_CONV_WIDTH), (0, 0)))
    n0_p = jnp.pad(n0, ((0, 0), (0, SUBLANES - nh), (0, 0)))
    m0_p = jnp.broadcast_to(jnp.pad(m0, ((0, 0), (0, SUBLANES - nh)))[:, :, None], (B, SUBLANES, LANES))
    wg = jnp.pad(jnp.concatenate([wi, wf], axis=1).T, ((0, GATE_ROWS - 2 * nh), (0, 0)))
    bg = jnp.pad(jnp.concatenate([bi, bf]), (0, GATE_ROWS - 2 * nh)).reshape(GATE_ROWS, 1)
    row = lambda v: v.reshape(1, -1)
    nt = di // MXU_TILE
    y, nhist, c_new, n_new, m_new = pl.pallas_call(
        functools.partial(_mlstm_kernel, L=L),
        grid=(B, T // L),
        in_specs=[
            pl.BlockSpec((1, L, d), lambda b, t: (b, t, 0)),
            pl.BlockSpec((1, MLSTM_CONV_PAD, di), lambda b, t: (b, 0, 0)),
            pl.BlockSpec((1, nh, dh, dh), lambda b, t: (b, 0, 0, 0)),
            pl.BlockSpec((1, SUBLANES, dh), lambda b, t: (b, 0, 0)),
            pl.BlockSpec((1, SUBLANES, LANES), lambda b, t: (b, 0, 0)),
            _full((1, d)), _full((d, 2 * di)), _full((SUBLANES, di)), _full((1, di)),
            _full((nt, MXU_TILE, MXU_TILE)), _full((nt, MXU_TILE, MXU_TILE)), _full((nt, MXU_TILE, MXU_TILE)),
            _full((GATE_ROWS, 3 * di)), _full((GATE_ROWS, 1)), _full((1, di)), _full((1, di)), _full((di, d)),
        ],
        out_specs=[
            pl.BlockSpec((1, L, d), lambda b, t: (b, t, 0)),
            pl.BlockSpec((1, MLSTM_CONV_PAD, di), lambda b, t: (b, 0, 0)),
            pl.BlockSpec((1, nh, dh, dh), lambda b, t: (b, 0, 0, 0)),
            pl.BlockSpec((1, SUBLANES, dh), lambda b, t: (b, 0, 0)),
            pl.BlockSpec((1, SUBLANES, LANES), lambda b, t: (b, 0, 0)),
        ],
        out_shape=[
            jax.ShapeDtypeStruct((B, T, d), F32),
            jax.ShapeDtypeStruct((B, MLSTM_CONV_PAD, di), F32),
            jax.ShapeDtypeStruct((B, nh, dh, dh), F32),
            jax.ShapeDtypeStruct((B, SUBLANES, dh), F32),
            jax.ShapeDtypeStruct((B, SUBLANES, LANES), F32),
        ],
        scratch_shapes=[
            pltpu.VMEM((MLSTM_CONV_PAD + L, di), F32),
            pltpu.VMEM((L, di), F32),
            pltpu.VMEM((L, 3 * di), F32),
            pltpu.VMEM((L, di), F32),
        ],
        compiler_params=pltpu.CompilerParams(dimension_semantics=("arbitrary", "arbitrary"),
                                             vmem_limit_bytes=VMEM_LIMIT),
        name="mlstm_mixer",
    )(x, hist_p, c0, n0_p, m0_p, row(g), wup.astype(MXU_DT), cw_p, row(cb),
      _blockdiag_tiles(wq).astype(MXU_DT), _blockdiag_tiles(wk).astype(MXU_DT),
      _blockdiag_tiles(wv).astype(MXU_DT), wg.astype(MXU_DT), bg, row(ng), row(sk), wd.astype(MXU_DT))
    return (y, nhist[:, MLSTM_CONV_PAD - MLSTM_CONV_HIST:, :], c_new, n_new[:, :nh, :], m_new[:, :nh, 0])


def _top16(tile):
    vals = []
    cur = tile
    for a in range(PEER_TOPK):
        mx = jnp.max(cur, axis=0, keepdims=True)
        vals.append(mx)
        if a + 1 < PEER_TOPK:
            cur = jnp.where(cur == mx, NEG_INF, cur)
    return vals


def _pack_heads(rows):
    sub = lax.broadcasted_iota(jnp.int32, (PEER_HEADS, LANES), 0)
    out = jnp.broadcast_to(rows[0], (PEER_HEADS, LANES))
    for hd in range(1, PEER_HEADS):
        out = jnp.where(sub == hd, rows[hd], out)
    return out


def _route_kernel(x_ref, g_ref, wqt_ref, k1_ref, k2_ref,
                  xt_ref, thr_ref, p1_ref, s2_ref, p2_ref, s1_s, *, tt):
    ng = tt // LANES
    xn = _rms(x_ref[...], g_ref[...])
    xt = xn.T.astype(MXU_DT)
    for gw in range(tt // PEER_WIDE):
        xt_ref[gw] = xt[:, gw * PEER_WIDE:(gw + 1) * PEER_WIDE]
    qt = jnp.dot(wqt_ref[...], xt, preferred_element_type=F32)
    for hd in range(PEER_HEADS):
        q1 = qt[hd * 2 * PEER_HALF:hd * 2 * PEER_HALF + PEER_HALF, :]
        q2 = qt[hd * 2 * PEER_HALF + PEER_HALF:(hd + 1) * 2 * PEER_HALF, :]
        s1 = _mm(k1_ref[hd], q1)
        s2 = _mm(k2_ref[hd], q2)
        for gi in range(ng):
            s1_s[gi, hd] = s1[:, gi * LANES:(gi + 1) * LANES]
            s2_ref[gi, hd] = s2[:, gi * LANES:(gi + 1) * LANES]

    pairs = [(a, b) for a in range(PEER_TOPK) for b in range(PEER_TOPK) if (a + 1) * (b + 1) <= PEER_TOPK]

    def group(gi, carry):
        v1 = [_top16(s1_s[gi, hd]) for hd in range(PEER_HEADS)]
        v2 = [_top16(s2_ref[gi, hd]) for hd in range(PEER_HEADS)]
        V1 = [_pack_heads([v1[hd][a] for hd in range(PEER_HEADS)]) for a in range(PEER_TOPK)]
        V2 = [_pack_heads([v2[hd][a] for hd in range(PEER_HEADS)]) for a in range(PEER_TOPK)]
        cand = [V1[a] + V2[b] for (a, b) in pairs]
        cur = list(cand)
        tau = None
        for it in range(PEER_TOPK):
            mx = cur[0]
            for cnd in cur[1:]:
                mx = jnp.maximum(mx, cnd)
            if it + 1 < PEER_TOPK:
                cur = [jnp.where(cnd == mx, NEG_INF, cnd) for cnd in cur]
            else:
                tau = mx
        E1 = [jnp.exp(V1[a] - V1[0]) for a in range(PEER_TOPK)]
        E2 = [jnp.exp(V2[b] - V2[0]) for b in range(PEER_TOPK)]
        zsum = jnp.zeros((PEER_HEADS, LANES), F32)
        for idx, (a, b) in enumerate(pairs):
            zsum = zsum + jnp.where(cand[idx] >= tau, E1[a] * E2[b], 0.0)
        inv_z = 1.0 / zsum
        for hd in range(PEER_HEADS):
            tau_h = tau[hd:hd + 1, :]
            s1 = s1_s[gi, hd]
            thr = jnp.full((PEER_N_KEYS, LANES), POS_INF, F32)
            for b in range(PEER_TOPK):
                vb = v2[hd][b]
                thr = jnp.minimum(thr, jnp.where(s1 + vb >= tau_h, vb, POS_INF))
            thr_ref[gi, hd] = thr
            p1_ref[gi, hd] = jnp.exp(s1 - v1[hd][0]) * inv_z[hd:hd + 1, :]
            p2_ref[gi, hd] = jnp.exp(s2_ref[gi, hd] - v2[hd][0])
        return carry

    lax.fori_loop(0, ng, group, 0)


def _peer_route(x, g, wqt, k1, k2, *, tt):
    n, d = x.shape
    assert n % tt == 0 and tt % LANES == 0
    ng = tt // LANES
    tile_spec = pl.BlockSpec((ng, PEER_HEADS, PEER_N_KEYS, LANES), lambda i: (i, 0, 0, 0))
    tile_shape = jax.ShapeDtypeStruct((n // LANES, PEER_HEADS, PEER_N_KEYS, LANES), F32)
    return pl.pallas_call(
        functools.partial(_route_kernel, tt=tt),
        grid=(n // tt,),
        in_specs=[
            pl.BlockSpec((tt, d), lambda i: (i, 0)),
            _full((1, d)), _full((2 * PEER_HALF * PEER_HEADS, d)),
            _full((PEER_HEADS, PEER_N_KEYS, PEER_HALF)), _full((PEER_HEADS, PEER_N_KEYS, PEER_HALF)),
        ],
        out_specs=[pl.BlockSpec((tt // PEER_WIDE, d, PEER_WIDE), lambda i: (i, 0, 0)),
                   tile_spec, tile_spec, tile_spec, tile_spec],
        out_shape=[jax.ShapeDtypeStruct((n // PEER_WIDE, d, PEER_WIDE), MXU_DT),
                   tile_shape, tile_shape, tile_shape, tile_shape],
        scratch_shapes=[pltpu.VMEM((ng, PEER_HEADS, PEER_N_KEYS, LANES), F32)],
        compiler_params=pltpu.CompilerParams(dimension_semantics=("arbitrary",),
                                             vmem_limit_bytes=VMEM_LIMIT),
        name="peer_route",
    )(x, g.reshape(1, d), wqt, k1, k2)


PEER_SUB = 512
PEER_WIDE = MXU_TILE
PEER_KEY_BLOCK = 32
PEER_J_BLOCK = 4
PEER_PIECES = D_MODEL // MXU_TILE
PEER_PIECES_PER_ITER = 2
PEER_OUT_ROWS = D_MODEL * (PEER_SUB // MXU_TILE) // PEER_PIECES
PEER_BLOCKS_PER_UNIT = ((PEER_WIDE // LANES) * (PEER_SUB // PEER_N_KEYS // PEER_J_BLOCK)
                        * (PEER_N_KEYS // PEER_KEY_BLOCK))
assert PEER_BLOCKS_PER_UNIT % PEER_PIECES == 0


def _gelu_tanh(x):
    return 0.5 * x * (1.0 + jnp.tanh(0.7978845608028654 * (x + 0.044715 * (x * x * x))))


def _expert_kernel(x_ref, xt_ref, thr_ref, p1_ref, s2_ref, p2_ref, u_ref, vt_ref, y_ref,
                   acc_s, s_a, s_b, w_a, w_b, *, tt, ec):
    s_s = (s_a, s_b)
    w_s = (w_a, w_b)
    c = pl.program_id(1)
    ngw = tt // PEER_WIDE
    halves = PEER_WIDE // LANES
    jn = PEER_SUB // PEER_N_KEYS
    kts = PEER_SUB // MXU_TILE
    nkb = PEER_N_KEYS // PEER_KEY_BLOCK
    njb = jn // PEER_J_BLOCK
    blocks_per_piece = halves * njb * nkb // PEER_PIECES

    @pl.when(c == 0)
    def _():
        acc_s[...] = jnp.zeros_like(acc_s)

    units = [(sub, gw) for sub in range(ec // PEER_SUB) for gw in range(ngw)]

    def score_piece(unit, buf, q):
        sub, gw = unit
        lhs = u_ref[q, sub * PEER_SUB:(sub + 1) * PEER_SUB, :]
        rhs = xt_ref[gw, pl.ds(pl.multiple_of(q * MXU_TILE, MXU_TILE), MXU_TILE), :]
        res = jnp.dot(lhs, rhs, preferred_element_type=F32)
        for half in range(halves):
            s_s[buf][half] += res[:, half * LANES:(half + 1) * LANES]

    def out_piece(unit, buf, q):
        sub, gw = unit
        kt = q % kts
        mh = q // kts
        m0 = pl.multiple_of(mh * PEER_OUT_ROWS, PEER_OUT_ROWS)
        e0 = pl.multiple_of(kt * MXU_TILE, MXU_TILE)
        lhs = vt_ref[sub * kts + kt, pl.ds(m0, PEER_OUT_ROWS), :]
        rhs = jnp.concatenate([w_s[buf][half, pl.ds(e0, MXU_TILE), :] for half in range(halves)], axis=1)
        acc_s[gw, pl.ds(m0, PEER_OUT_ROWS), :] += jnp.dot(lhs, rhs, preferred_element_type=F32)

    def gate_block(unit, buf, blk):
        sub, gw = unit
        half = blk // (njb * nkb)
        jj0 = ((blk // nkb) % njb) * PEER_J_BLOCK
        gi = gw * halves + half
        k0 = pl.multiple_of((blk % nkb) * PEER_KEY_BLOCK, PEER_KEY_BLOCK)
        gate = [None] * PEER_J_BLOCK
        for hd in range(PEER_HEADS):
            s2 = s2_ref[gi, hd, pl.ds(k0, PEER_KEY_BLOCK), :]
            p2 = p2_ref[gi, hd, pl.ds(k0, PEER_KEY_BLOCK), :]
            for jo in range(PEER_J_BLOCK):
                j = sub * jn + jj0 + jo
                thr = thr_ref[gi, hd, pl.ds(j, 1), :]
                p1 = p1_ref[gi, hd, pl.ds(j, 1), :]
                term = jnp.where(s2 >= thr, p2, 0.0) * p1
                gate[jo] = term if gate[jo] is None else gate[jo] + term
        for jo in range(PEER_J_BLOCK):
            rows = pl.ds(pl.multiple_of((jj0 + jo) * PEER_N_KEYS + k0, PEER_KEY_BLOCK), PEER_KEY_BLOCK)
            act = _gelu_tanh(s_s[buf][half, rows, :])
            w_s[buf][half, rows, :] = (gate[jo] * act).astype(MXU_DT)

    s_s[0][...] = jnp.zeros_like(s_s[0])
    for q in range(PEER_PIECES):
        score_piece(units[0], 0, q)
    for k, unit in enumerate(units):
        buf = k % 2
        prev_unit = units[k - 1] if k > 0 else None
        next_unit = units[k + 1] if k + 1 < len(units) else None
        if next_unit is not None:
            s_s[1 - buf][...] = jnp.zeros_like(s_s[1 - buf])

        def body(it, carry, unit=unit, buf=buf, prev_unit=prev_unit, next_unit=next_unit):
            for r in range(PEER_PIECES_PER_ITER):
                q = it * PEER_PIECES_PER_ITER + r
                if prev_unit is not None:
                    out_piece(prev_unit, 1 - buf, q)
                if next_unit is not None:
                    score_piece(next_unit, 1 - buf, q)
            for r in range(PEER_PIECES_PER_ITER):
                q = it * PEER_PIECES_PER_ITER + r
                for bs in range(blocks_per_piece):
                    gate_block(unit, buf, q * blocks_per_piece + bs)
            return carry

        lax.fori_loop(0, PEER_PIECES // PEER_PIECES_PER_ITER, body, 0)
    for q in range(PEER_PIECES):
        out_piece(units[-1], (len(units) - 1) % 2, q)

    @pl.when(c == pl.num_programs(1) - 1)
    def _():
        for gw in range(ngw):
            rows = slice(gw * PEER_WIDE, (gw + 1) * PEER_WIDE)
            y_ref[rows, :] = x_ref[rows, :] + acc_s[gw].T


def _peer_experts(x, xt, thr, p1, s2, p2, u4, vt4, *, tt, ec):
    n, d = x.shape
    ne = u4.shape[1]
    assert n % tt == 0 and tt % PEER_WIDE == 0 and ne % ec == 0 and ec % PEER_SUB == 0
    ng = tt // LANES
    ngw = tt // PEER_WIDE
    jc = ec // PEER_N_KEYS
    halves = PEER_WIDE // LANES
    key_chunk = pl.BlockSpec((ng, PEER_HEADS, jc, LANES), lambda i, c: (i, 0, c, 0))
    all_keys = pl.BlockSpec((ng, PEER_HEADS, PEER_N_KEYS, LANES), lambda i, c: (i, 0, 0, 0))
    return pl.pallas_call(
        functools.partial(_expert_kernel, tt=tt, ec=ec),
        grid=(n // tt, ne // ec),
        in_specs=[
            pl.BlockSpec((tt, d), lambda i, c: (i, 0)),
            pl.BlockSpec((ngw, d, PEER_WIDE), lambda i, c: (i, 0, 0)),
            key_chunk, key_chunk, all_keys, all_keys,
            pl.BlockSpec((d // MXU_TILE, ec, MXU_TILE), lambda i, c: (0, c, 0)),
            pl.BlockSpec((ec // MXU_TILE, d, MXU_TILE), lambda i, c: (c, 0, 0)),
        ],
        out_specs=pl.BlockSpec((tt, d), lambda i, c: (i, 0)),
        out_shape=jax.ShapeDtypeStruct((n, d), F32),
        scratch_shapes=[pltpu.VMEM((ngw, d, PEER_WIDE), F32),
                        pltpu.VMEM((halves, PEER_SUB, LANES), F32), pltpu.VMEM((halves, PEER_SUB, LANES), F32),
                        pltpu.VMEM((halves, PEER_SUB, LANES), MXU_DT),
                        pltpu.VMEM((halves, PEER_SUB, LANES), MXU_DT)],
        compiler_params=pltpu.CompilerParams(dimension_semantics=("arbitrary", "arbitrary"),
                                             vmem_limit_bytes=VMEM_LIMIT),
        name="peer_experts",
    )(x, xt, thr, p1, s2, p2, u4, vt4)


def _peer_weights(wq, k1, k2, u, v):
    ne, d = u.shape
    u4 = u.astype(MXU_DT).reshape(ne, d // MXU_TILE, MXU_TILE).transpose(1, 0, 2)
    vt4 = v.astype(MXU_DT).reshape(ne // MXU_TILE, MXU_TILE, d).transpose(0, 2, 1)
    return wq.T.astype(MXU_DT), k1.astype(MXU_DT), k2.astype(MXU_DT), u4, vt4


def _peer_layer(x, g, weights, *, tt_route, tt, ec):
    wqt, k1, k2, u4, vt4 = weights
    shape = x.shape
    x = x.reshape(-1, shape[-1])
    xt, thr, p1, s2, p2 = _peer_route(x, g, wqt, k1, k2, tt=tt_route)
    return _peer_experts(x, xt, thr, p1, s2, p2, u4, vt4, tt=tt, ec=ec).reshape(shape)


def _norm_kernel(x_ref, g_ref, y_ref):
    y_ref[...] = _rms(x_ref[...], g_ref[...])


def _final_norm(x, g, *, tt):
    n, d = x.shape
    return pl.pallas_call(
        _norm_kernel,
        grid=(n // tt,),
        in_specs=[pl.BlockSpec((tt, d), lambda i: (i, 0)), _full((1, d))],
        out_specs=pl.BlockSpec((tt, d), lambda i: (i, 0)),
        out_shape=jax.ShapeDtypeStruct((n, d), F32),
        name="final_norm",
    )(x, g.reshape(1, d))


CONV_PROMPT_TILE = 512
CONV_SAMPLE_SEQS = 8
PEER_ROUTE_TILE = 512
PEER_TOKEN_TILE = 512
PEER_EXPERT_CHUNK = 2048
NORM_TILE = 1024


def kernel(x_prompt, x_sample, cache_conv, state_mlstm_conv, state_C, state_n, state_m, norm_mix, norm_ffn, norm_final, cm_w1, cm_b1, cm_dw, cm_dwb, cm_ln_g, cm_ln_b, cm_w2, cm_b2, ml_wup, ml_convw, ml_convb, ml_wq, ml_wk, ml_wv, ml_wi, ml_bi, ml_wf, ml_bf, ml_norm, ml_skip, ml_wdown, pk_wq, pk_k1, pk_k2, pk_u, pk_v):
    xp, xs = x_prompt, x_sample
    bp, tp, d = xp.shape
    bs, ts, _ = xs.shape
    di, nh, dh = MLSTM_INNER, MLSTM_HEADS, MLSTM_HEAD_DIM
    conv_p, conv_s, mconv_p, mconv_s = [], [], [], []
    cp_l, cs_l, np_l, ns_l, mp_l, ms_l = [], [], [], [], [], []
    for layer in range(DEPTH):
        j = layer // N_MIXERS
        if layer % N_MIXERS == 0:
            cw = (norm_mix[layer], cm_w1[j], cm_b1[j], cm_dw[j], cm_dwb[j], cm_ln_g[j], cm_ln_b[j],
                  cm_w2[j], cm_b2[j])
            xp, hist_p = _conv_layer(xp, jnp.zeros((bp, CONV_HIST, d), F32), *cw, bb=1,
                                     tt=min(tp, CONV_PROMPT_TILE))
            xs, hist_s = _conv_layer(xs, cache_conv[j], *cw, bb=min(bs, CONV_SAMPLE_SEQS), tt=ts)
            conv_p.append(hist_p)
            conv_s.append(hist_s)
        else:
            mw = (norm_mix[layer], ml_wup[j], ml_convw[j], ml_convb[j], ml_wq[j], ml_wk[j], ml_wv[j],
                  ml_wi[j], ml_bi[j], ml_wf[j], ml_bf[j], ml_norm[j], ml_skip[j], ml_wdown[j])
            xp, hcp, c_p, n_p, m_p = _mlstm_layer(
                xp, jnp.zeros((bp, MLSTM_CONV_HIST, di), F32), jnp.zeros((bp, nh, dh, dh), F32),
                jnp.zeros((bp, nh, dh), F32), jnp.zeros((bp, nh), F32), *mw,
                chunk=min(tp, MLSTM_PROMPT_CHUNK))
            xs, hcs, c_s, n_s, m_s = _mlstm_layer(
                xs, state_mlstm_conv[j], state_C[j], state_n[j], state_m[j], *mw, chunk=ts)
            mconv_p.append(hcp)
            mconv_s.append(hcs)
            cp_l.append(c_p)
            cs_l.append(c_s)
            np_l.append(n_p)
            ns_l.append(n_s)
            mp_l.append(m_p)
            ms_l.append(m_s)
        pw = _peer_weights(pk_wq[layer], pk_k1[layer], pk_k2[layer], pk_u[layer], pk_v[layer])
        tiles = dict(tt_route=PEER_ROUTE_TILE, tt=PEER_TOKEN_TILE, ec=PEER_EXPERT_CHUNK)
        xp = _peer_layer(xp, norm_ffn[layer], pw, **tiles)
        xs = _peer_layer(xs, norm_ffn[layer], pw, **tiles)
    xp = _final_norm(xp.reshape(bp * tp, d), norm_final, tt=NORM_TILE).reshape(bp, tp, d)
    xs = _final_norm(xs.reshape(bs * ts, d), norm_final, tt=NORM_TILE).reshape(bs, ts, d)
    return (xp, xs,
            jnp.stack(conv_p), jnp.stack(conv_s),
            jnp.stack(mconv_p), jnp.stack(mconv_s),
            jnp.stack(cp_l), jnp.stack(cs_l),
            jnp.stack(np_l), jnp.stack(ns_l),
            jnp.stack(mp_l), jnp.stack(ms_l))
```

```python
import functools

import jax
import jax.numpy as jnp
from jax import lax
from jax.experimental import pallas as pl
from jax.experimental.pallas import tpu as pltpu

F32 = jnp.float32
MXU_DT = jnp.bfloat16

D_MODEL = 1024
DEPTH = 4
N_MIXERS = 2
CONV_WIDTH = 31
CONV_HIST = CONV_WIDTH - 1
CONV_PAD = 32
MLSTM_INNER = 2 * D_MODEL
MLSTM_HEADS = 4
MLSTM_HEAD_DIM = MLSTM_INNER // MLSTM_HEADS
MLSTM_QKV_BLOCK = 4
MLSTM_CONV_WIDTH = 4
MLSTM_CONV_HIST = MLSTM_CONV_WIDTH - 1
MLSTM_CONV_PAD = 8
MLSTM_PROMPT_CHUNK = 256
PEER_HEADS = 8
PEER_N_KEYS = 128
PEER_N_EXPERTS = PEER_N_KEYS * PEER_N_KEYS
PEER_TOPK = 16
PEER_HALF = 128
RMS_EPS = 1e-6
LN_EPS = 1e-5

LANES = 128
SUBLANES = 8
MXU_TILE = 256
GATE_ROWS = 16
VMEM_LIMIT = 56 * 1024 * 1024
NEG_INF = float("-inf")
POS_INF = float("inf")


def _full(shape):
    n = len(shape)
    return pl.BlockSpec(shape, lambda *_: (0,) * n)


def _rms(x, g):
    return x * lax.rsqrt(jnp.mean(x * x, axis=-1, keepdims=True) + RMS_EPS) * g


def _mm(a, b):
    return jnp.dot(a.astype(MXU_DT), b.astype(MXU_DT), preferred_element_type=F32)


def _mm_nt(a, b):
    return lax.dot_general(a.astype(MXU_DT), b.astype(MXU_DT), (((1,), (1,)), ((), ())),
                           preferred_element_type=F32)


CONV_ROW_BLOCK = 32


def _conv_kernel(x_ref, hist_ref, g_ref, w1_ref, b1_ref, dw_ref, dwb_ref, lng_ref, lnb_ref, w2_ref, b2_ref,
                 y_ref, nh_ref, ubuf, cbuf, *, bb, tt):
    d = D_MODEL
    t = pl.program_id(1)

    @pl.when(t == 0)
    def _():
        ubuf[:, 0:CONV_PAD, :] = hist_ref[...]

    x = x_ref[...].reshape(bb * tt, d)
    h = _rms(x, g_ref[...])
    a = _mm(h, w1_ref[...]) + b1_ref[...]
    u = a[:, :d] * jax.nn.sigmoid(a[:, d:])
    ubuf[:, CONV_PAD:CONV_PAD + tt, :] = u.reshape(bb, tt, d)

    nrb = tt // CONV_ROW_BLOCK
    lead = CONV_PAD - CONV_HIST

    def row_block(i, carry):
        b = i // nrb
        r0 = pl.multiple_of((i % nrb) * CONV_ROW_BLOCK, CONV_ROW_BLOCK)
        for c in range(d // LANES):
            cs = slice(c * LANES, (c + 1) * LANES)
            win = ubuf[b, pl.ds(r0, CONV_ROW_BLOCK + CONV_PAD), cs]
            wts = dw_ref[:, cs]
            acc = jnp.zeros((CONV_ROW_BLOCK, LANES), F32)
            for k in range(CONV_WIDTH):
                acc = acc + win[lead + k:lead + k + CONV_ROW_BLOCK, :] * wts[k:k + 1, :]
            cbuf[b, pl.ds(r0, CONV_ROW_BLOCK), cs] = acc
        return carry

    lax.fori_loop(0, bb * nrb, row_block, 0)

    c = cbuf[...].reshape(bb * tt, d) + dwb_ref[...]
    mu = jnp.mean(c, axis=-1, keepdims=True)
    var = jnp.mean(jnp.square(c - mu), axis=-1, keepdims=True)
    c = (c - mu) * lax.rsqrt(var + LN_EPS) * lng_ref[...] + lnb_ref[...]
    c = c * jax.nn.sigmoid(c)
    y = _mm(c, w2_ref[...]) + b2_ref[...]
    y_ref[...] = (x + y).reshape(bb, tt, d)

    new_hist = ubuf[:, tt:tt + CONV_PAD, :]
    nh_ref[...] = new_hist
    ubuf[:, 0:CONV_PAD, :] = new_hist


def _conv_layer(x, hist, g, w1, b1, dw, dwb, lng, lnb, w2, b2, *, bb, tt):
    B, T, d = x.shape
    assert B % bb == 0 and T % tt == 0 and tt % CONV_ROW_BLOCK == 0 and tt >= CONV_PAD
    hist_p = jnp.pad(hist, ((0, 0), (CONV_PAD - CONV_HIST, 0), (0, 0)))
    dw_p = jnp.pad(dw, ((0, CONV_PAD - CONV_WIDTH), (0, 0)))
    row = lambda v: v.reshape(1, -1)
    y, nh = pl.pallas_call(
        functools.partial(_conv_kernel, bb=bb, tt=tt),
        grid=(B // bb, T // tt),
        in_specs=[
            pl.BlockSpec((bb, tt, d), lambda b, t: (b, t, 0)),
            pl.BlockSpec((bb, CONV_PAD, d), lambda b, t: (b, 0, 0)),
            _full((1, d)), _full((d, 2 * d)), _full((1, 2 * d)), _full((CONV_PAD, d)), _full((1, d)),
            _full((1, d)), _full((1, d)), _full((d, d)), _full((1, d)),
        ],
        out_specs=[
            pl.BlockSpec((bb, tt, d), lambda b, t: (b, t, 0)),
            pl.BlockSpec((bb, CONV_PAD, d), lambda b, t: (b, 0, 0)),
        ],
        out_shape=[jax.ShapeDtypeStruct((B, T, d), F32), jax.ShapeDtypeStruct((B, CONV_PAD, d), F32)],
        scratch_shapes=[pltpu.VMEM((bb, CONV_PAD + tt, d), F32), pltpu.VMEM((bb, tt, d), F32)],
        compiler_params=pltpu.CompilerParams(dimension_semantics=("arbitrary", "arbitrary"),
                                             vmem_limit_bytes=VMEM_LIMIT),
        name="conv_mixer",
    )(x, hist_p, row(g), w1.astype(MXU_DT), row(b1), dw_p, row(dwb), row(lng), row(lnb),
      w2.astype(MXU_DT), row(b2))
    return y, nh[:, CONV_PAD - CONV_HIST:, :]


MLSTM_ROW_BLOCK = 32


def _mlstm_kernel(x_ref, hist_ref, c0_ref, n0_ref, m0_ref, g_ref, wup_ref, cw_ref, cb_ref, wq_ref, wk_ref,
                  wv_ref, wg_ref, bg_ref, ng_ref, sk_ref, wd_ref,
                  y_ref, nh_ref, c_ref, n_ref, m_ref, cbuf, xc_s, qkv_s, hh_s, *, L):
    d, di, nh, dh = D_MODEL, MLSTM_INNER, MLSTM_HEADS, MLSTM_HEAD_DIM
    t = pl.program_id(1)

    @pl.when(t == 0)
    def _():
        cbuf[0:MLSTM_CONV_PAD, :] = hist_ref[0]
        c_ref[...] = c0_ref[...]
        n_ref[...] = n0_ref[...]
        m_ref[...] = m0_ref[...]

    x = x_ref[0]
    h = _rms(x, g_ref[...])
    up = _mm(h, wup_ref[...])
    xm = up[:, :di]
    z = up[:, di:]
    cbuf[MLSTM_CONV_PAD:MLSTM_CONV_PAD + L, :] = xm

    lead = MLSTM_CONV_PAD - MLSTM_CONV_HIST

    def row_block(i, carry):
        r0 = pl.multiple_of(i * MLSTM_ROW_BLOCK, MLSTM_ROW_BLOCK)
        for c in range(di // LANES):
            cs = slice(c * LANES, (c + 1) * LANES)
            win = cbuf[pl.ds(r0, MLSTM_ROW_BLOCK + MLSTM_CONV_PAD), cs]
            wts = cw_ref[:, cs]
            acc = jnp.zeros((MLSTM_ROW_BLOCK, LANES), F32) + cb_ref[:, cs]
            for k in range(MLSTM_CONV_WIDTH):
                acc = acc + win[lead + k:lead + k + MLSTM_ROW_BLOCK, :] * wts[k:k + 1, :]
            xc_s[pl.ds(r0, MLSTM_ROW_BLOCK), cs] = acc * jax.nn.sigmoid(acc)
        return carry

    lax.fori_loop(0, L // MLSTM_ROW_BLOCK, row_block, 0)
    new_hist = cbuf[L:L + MLSTM_CONV_PAD, :]
    nh_ref[0] = new_hist
    cbuf[0:MLSTM_CONV_PAD, :] = new_hist

    xc = xc_s[...]
    for c in range(di // MXU_TILE):
        cs = slice(c * MXU_TILE, (c + 1) * MXU_TILE)
        qkv_s[:, c * MXU_TILE:(c + 1) * MXU_TILE] = _mm(xc[:, cs], wq_ref[c])
        qkv_s[:, di + c * MXU_TILE:di + (c + 1) * MXU_TILE] = _mm(xc[:, cs], wk_ref[c])
        qkv_s[:, 2 * di + c * MXU_TILE:2 * di + (c + 1) * MXU_TILE] = _mm(xm[:, cs], wv_ref[c])

    gates = _mm_nt(wg_ref[...], qkv_s[...]) + bg_ref[...]
    ig_all = gates[0:nh, :]
    fp = gates[0:SUBLANES, :]
    lf_all = jnp.minimum(fp, 0.0) - jnp.log(1.0 + jnp.exp(-jnp.abs(fp)))
    ri = lax.broadcasted_iota(jnp.int32, (L, L), 0)
    ci = lax.broadcasted_iota(jnp.int32, (L, L), 1)
    causal = ri >= ci
    eye = ri == ci
    tri = jnp.where(ri <= ci, 1.0, 0.0).astype(F32)
    b_all = jnp.dot(lf_all, tri, preferred_element_type=F32, precision=lax.Precision.HIGHEST)

    def to_col(row):
        return jnp.sum(jnp.where(eye, row, 0.0), axis=1, keepdims=True)

    kscale = dh ** -0.5
    for hd in range(nh):
        hs = slice(hd * dh, (hd + 1) * dh)
        qh = qkv_s[:, hd * dh:(hd + 1) * dh]
        kh = qkv_s[:, di + hd * dh:di + (hd + 1) * dh] * kscale
        vh = qkv_s[:, 2 * di + hd * dh:2 * di + (hd + 1) * dh]
        ig_r = ig_all[hd:hd + 1, :]
        b_r = b_all[nh + hd:nh + hd + 1, :]
        b_c = to_col(b_r)
        m_prev = m_ref[0, hd:hd + 1, 0:1]
        n_r = n_ref[0, hd:hd + 1, :]
        cmat = c_ref[0, 0, hd]

        logw = jnp.where(causal, b_c - b_r + ig_r, NEG_INF)
        inter = b_c + m_prev
        m_row = jnp.maximum(jnp.max(logw, axis=1, keepdims=True), inter)
        w_intra = jnp.exp(logw - m_row)
        w_inter = jnp.exp(inter - m_row)
        s = _mm_nt(qh, kh) * w_intra
        num = _mm(s, vh) + w_inter * _mm(qh, cmat)
        den = jnp.sum(s, axis=1, keepdims=True) + w_inter * jnp.sum(qh * n_r, axis=1, keepdims=True)
        hh = num / jnp.maximum(jnp.abs(den), jnp.exp(-m_row))

        mu = jnp.mean(hh, axis=-1, keepdims=True)
        var = jnp.mean(jnp.square(hh - mu), axis=-1, keepdims=True)
        hh_s[:, hs] = (hh - mu) * lax.rsqrt(var + LN_EPS)

        b_end = b_r[:, L - 1:L]
        g_r = b_end - b_r + ig_r
        m_new = jnp.maximum(b_end + m_prev, jnp.max(g_r, axis=1, keepdims=True))
        wk_c = to_col(jnp.exp(g_r - m_new))
        decay = jnp.exp(b_end + m_prev - m_new)
        kw = kh * wk_c
        upd = lax.dot_general(kw.astype(MXU_DT), vh.astype(MXU_DT), (((0,), (0,)), ((), ())),
                              preferred_element_type=F32)
        c_ref[0, 0, hd] = decay * cmat + upd
        n_ref[0, hd:hd + 1, :] = decay * n_r + jnp.sum(kw, axis=0, keepdims=True)
        m_ref[0, hd:hd + 1, :] = jnp.broadcast_to(m_new, (1, LANES))

    hn = hh_s[...] * ng_ref[...]
    out = jax.nn.sigmoid(z) * (hn + sk_ref[...] * xc)
    y_ref[0] = x + _mm(out, wd_ref[...])


def _blockdiag_tiles(w):
    g, bs, _ = w.shape
    per = MXU_TILE // bs
    wt = w.reshape(g // per, per, bs, bs)
    eye = jnp.eye(per, dtype=w.dtype)
    dense = jnp.einsum("tpij,pq->tpiqj", wt, eye)
    return dense.reshape(g // per, MXU_TILE, MXU_TILE)


def _mlstm_kernel_into(c_all_ref, *refs, L):
    del c_all_ref
    _mlstm_kernel(*refs, L=L)


def _mlstm_layer(x, hist, c0, slot_in, c_out, slot_out, n_slots, n0, m0,
                 g, wup, cw, cb, wq, wk, wv, wi, bi, wf, bf, ng, sk, wd, *, chunk):
    B, T, d = x.shape
    di, nh, dh, L = MLSTM_INNER, MLSTM_HEADS, MLSTM_HEAD_DIM, chunk
    assert T % L == 0 and L % MLSTM_ROW_BLOCK == 0
    hist_p = jnp.pad(hist, ((0, 0), (MLSTM_CONV_PAD - MLSTM_CONV_HIST, 0), (0, 0)))
    cw_p = jnp.pad(cw, ((0, SUBLANES - MLSTM_CONV_WIDTH), (0, 0)))
    n0_p = jnp.pad(n0, ((0, 0), (0, SUBLANES - nh), (0, 0)))
    m0_p = jnp.broadcast_to(jnp.pad(m0, ((0, 0), (0, SUBLANES - nh)))[:, :, None], (B, SUBLANES, LANES))
    wg = jnp.pad(jnp.concatenate([wi, wf], axis=1).T, ((0, GATE_ROWS - 2 * nh), (0, 0)))
    bg = jnp.pad(jnp.concatenate([bi, bf]), (0, GATE_ROWS - 2 * nh)).reshape(GATE_ROWS, 1)
    row = lambda v: v.reshape(1, -1)
    nt = di // MXU_TILE
    into = c_out is not None
    y, nhist, c_new, n_new, m_new = pl.pallas_call(
        functools.partial(_mlstm_kernel_into if into else _mlstm_kernel, L=L),
        grid=(B, T // L),
        in_specs=([pl.BlockSpec(memory_space=pl.ANY)] if into else []) + [
            pl.BlockSpec((1, L, d), lambda b, t: (b, t, 0)),
            pl.BlockSpec((1, MLSTM_CONV_PAD, di), lambda b, t: (b, 0, 0)),
            pl.BlockSpec((1, 1, nh, dh, dh), lambda b, t: (slot_in, b, 0, 0, 0)),
            pl.BlockSpec((1, SUBLANES, dh), lambda b, t: (b, 0, 0)),
            pl.BlockSpec((1, SUBLANES, LANES), lambda b, t: (b, 0, 0)),
            _full((1, d)), _full((d, 2 * di)), _full((SUBLANES, di)), _full((1, di)),
            _full((nt, MXU_TILE, MXU_TILE)), _full((nt, MXU_TILE, MXU_TILE)), _full((nt, MXU_TILE, MXU_TILE)),
            _full((GATE_ROWS, 3 * di)), _full((GATE_ROWS, 1)), _full((1, di)), _full((1, di)), _full((di, d)),
        ],
        out_specs=[
            pl.BlockSpec((1, L, d), lambda b, t: (b, t, 0)),
            pl.BlockSpec((1, MLSTM_CONV_PAD, di), lambda b, t: (b, 0, 0)),
            pl.BlockSpec((1, 1, nh, dh, dh), lambda b, t: (slot_out, b, 0, 0, 0)),
            pl.BlockSpec((1, SUBLANES, dh), lambda b, t: (b, 0, 0)),
            pl.BlockSpec((1, SUBLANES, LANES), lambda b, t: (b, 0, 0)),
        ],
        out_shape=[
            jax.ShapeDtypeStruct((B, T, d), F32),
            jax.ShapeDtypeStruct((B, MLSTM_CONV_PAD, di), F32),
            jax.ShapeDtypeStruct((n_slots, B, nh, dh, dh), F32),
            jax.ShapeDtypeStruct((B, SUBLANES, dh), F32),
            jax.ShapeDtypeStruct((B, SUBLANES, LANES), F32),
        ],
        scratch_shapes=[
            pltpu.VMEM((MLSTM_CONV_PAD + L, di), F32),
            pltpu.VMEM((L, di), F32),
            pltpu.VMEM((L, 3 * di), F32),
            pltpu.VMEM((L, di), F32),
        ],
        compiler_params=pltpu.CompilerParams(dimension_semantics=("arbitrary", "arbitrary"),
                                             vmem_limit_bytes=VMEM_LIMIT),
        input_output_aliases={0: 2} if into else {},
        name="mlstm_mixer",
    )(*((c_out,) if into else ()), x, hist_p, c0, n0_p, m0_p, row(g), wup.astype(MXU_DT), cw_p, row(cb),
      _blockdiag_tiles(wq).astype(MXU_DT), _blockdiag_tiles(wk).astype(MXU_DT),
      _blockdiag_tiles(wv).astype(MXU_DT), wg.astype(MXU_DT), bg, row(ng), row(sk), wd.astype(MXU_DT))
    return (y, nhist[:, MLSTM_CONV_PAD - MLSTM_CONV_HIST:, :], c_new, n_new[:, :nh, :], m_new[:, :nh, 0])


def _sorting_network(n):
    pairs = []
    p = 1
    while p < n:
        k = p
        while k >= 1:
            for j in range(k % p, n - k, 2 * k):
                for i in range(min(k, n - j - k)):
                    if (i + j) // (2 * p) == (i + j + k) // (2 * p):
                        pairs.append((i + j, i + j + k))
            k //= 2
        p *= 2
    return pairs


def _top16(tile):
    nrow = PEER_N_KEYS // SUBLANES
    c = [tile[r * SUBLANES:(r + 1) * SUBLANES, :] for r in range(nrow)]
    for i, j in _sorting_network(nrow):
        c[i], c[j] = jnp.maximum(c[i], c[j]), jnp.minimum(c[i], c[j])
    vals = []
    for a in range(PEER_TOPK):
        mx = jnp.max(c[0], axis=0, keepdims=True)
        vals.append(mx)
        if a + 1 < PEER_TOPK:
            hit = c[0] == mx
            for i in range(PEER_TOPK - 1 - a):
                c[i] = jnp.where(hit, c[i + 1], c[i])
    return vals


def _pack_heads(rows):
    sub = lax.broadcasted_iota(jnp.int32, (PEER_HEADS, LANES), 0)
    out = jnp.broadcast_to(rows[0], (PEER_HEADS, LANES))
    for hd in range(1, PEER_HEADS):
        out = jnp.where(sub == hd, rows[hd], out)
    return out


def _route_kernel(x_ref, g_ref, wqt_ref, k1_ref, k2_ref,
                  xt_ref, thr_ref, p1_ref, s2_ref, p2_ref, s1_s, *, tt):
    ng = tt // LANES
    xn = _rms(x_ref[...], g_ref[...])
    xt = xn.T.astype(MXU_DT)
    for gw in range(tt // PEER_WIDE):
        xt_ref[gw] = xt[:, gw * PEER_WIDE:(gw + 1) * PEER_WIDE]
    qt = jnp.dot(wqt_ref[...], xt, preferred_element_type=F32)
    for hd in range(PEER_HEADS):
        q1 = qt[hd * 2 * PEER_HALF:hd * 2 * PEER_HALF + PEER_HALF, :]
        q2 = qt[hd * 2 * PEER_HALF + PEER_HALF:(hd + 1) * 2 * PEER_HALF, :]
        s1 = _mm(k1_ref[hd], q1)
        s2 = _mm(k2_ref[hd], q2)
        for gi in range(ng):
            s1_s[gi, hd] = s1[:, gi * LANES:(gi + 1) * LANES]
            s2_ref[gi, hd] = s2[:, gi * LANES:(gi + 1) * LANES]

    pairs = [(a, b) for a in range(PEER_TOPK) for b in range(PEER_TOPK) if (a + 1) * (b + 1) <= PEER_TOPK]

    def group(gi, carry):
        v1 = [_top16(s1_s[gi, hd]) for hd in range(PEER_HEADS)]
        v2 = [_top16(s2_ref[gi, hd]) for hd in range(PEER_HEADS)]
        V1 = [_pack_heads([v1[hd][a] for hd in range(PEER_HEADS)]) for a in range(PEER_TOPK)]
        V2 = [_pack_heads([v2[hd][a] for hd in range(PEER_HEADS)]) for a in range(PEER_TOPK)]
        cand = [V1[a] + V2[b] for (a, b) in pairs]
        cur = list(cand)
        tau = None
        for it in range(PEER_TOPK):
            mx = cur[0]
            for cnd in cur[1:]:
                mx = jnp.maximum(mx, cnd)
            if it + 1 < PEER_TOPK:
                cur = [jnp.where(cnd == mx, NEG_INF, cnd) for cnd in cur]
            else:
                tau = mx
        E1 = [jnp.exp(V1[a] - V1[0]) for a in range(PEER_TOPK)]
        E2 = [jnp.exp(V2[b] - V2[0]) for b in range(PEER_TOPK)]
        zsum = jnp.zeros((PEER_HEADS, LANES), F32)
        for idx, (a, b) in enumerate(pairs):
            zsum = zsum + jnp.where(cand[idx] >= tau, E1[a] * E2[b], 0.0)
        inv_z = 0.5 / zsum
        for hd in range(PEER_HEADS):
            tau_h = tau[hd:hd + 1, :]
            s1 = s1_s[gi, hd]
            thr = jnp.full((PEER_N_KEYS, LANES), POS_INF, F32)
            for b in range(PEER_TOPK):
                vb = v2[hd][b]
                thr = jnp.minimum(thr, jnp.where(s1 + vb >= tau_h, vb, POS_INF))
            thr_ref[gi, hd] = thr
            p1_ref[gi, hd] = jnp.exp(s1 - v1[hd][0]) * inv_z[hd:hd + 1, :]
            p2_ref[gi, hd] = jnp.exp(s2_ref[gi, hd] - v2[hd][0])
        return carry

    lax.fori_loop(0, ng, group, 0)


def _peer_route(x, g, wqt, k1, k2, *, tt):
    n, d = x.shape
    assert n % tt == 0 and tt % PEER_WIDE == 0
    ng = tt // LANES
    tile_spec = pl.BlockSpec((ng, PEER_HEADS, PEER_N_KEYS, LANES), lambda i: (i, 0, 0, 0))
    tile_shape = jax.ShapeDtypeStruct((n // LANES, PEER_HEADS, PEER_N_KEYS, LANES), F32)
    return pl.pallas_call(
        functools.partial(_route_kernel, tt=tt),
        grid=(n // tt,),
        in_specs=[
            pl.BlockSpec((tt, d), lambda i: (i, 0)),
            _full((1, d)), _full((2 * PEER_HALF * PEER_HEADS, d)),
            _full((PEER_HEADS, PEER_N_KEYS, PEER_HALF)), _full((PEER_HEADS, PEER_N_KEYS, PEER_HALF)),
        ],
        out_specs=[pl.BlockSpec((tt // PEER_WIDE, d, PEER_WIDE), lambda i: (i, 0, 0)),
                   tile_spec, tile_spec, tile_spec, tile_spec],
        out_shape=[jax.ShapeDtypeStruct((n // PEER_WIDE, d, PEER_WIDE), MXU_DT),
                   tile_shape, tile_shape, tile_shape, tile_shape],
        scratch_shapes=[pltpu.VMEM((ng, PEER_HEADS, PEER_N_KEYS, LANES), F32)],
        compiler_params=pltpu.CompilerParams(dimension_semantics=("arbitrary",),
                                             vmem_limit_bytes=VMEM_LIMIT),
        name="peer_route",
    )(x, g.reshape(1, d), wqt, k1, k2)


PEER_SUB = 512
PEER_WIDE = MXU_TILE
PEER_KEY_BLOCK = 32
PEER_J_BLOCK = 4
PEER_PIECES = D_MODEL // MXU_TILE
PEER_OUT_ROWS = 512
PEER_BLOCKS_PER_UNIT = ((PEER_WIDE // LANES) * (PEER_SUB // PEER_N_KEYS // PEER_J_BLOCK)
                        * (PEER_N_KEYS // PEER_KEY_BLOCK))


GELU_C0 = 0.7978845608028654
GELU_C1 = 0.044715 * GELU_C0


def _twice_gelu_tanh(x):
    return x * (1.0 + jnp.tanh(x * (GELU_C0 + GELU_C1 * (x * x))))


def _expert_kernel(x_ref, xt_ref, thr_ref, p1_ref, s2_ref, p2_ref, u_ref, vt_ref, y_ref,
                   acc_s, s_a, s_b, w_a, w_b, *, tt, ec):
    s_s = (s_a, s_b)
    w_s = (w_a, w_b)
    c = pl.program_id(1)
    ngw = tt // PEER_WIDE
    halves = PEER_WIDE // LANES
    jn = PEER_SUB // PEER_N_KEYS
    kts = PEER_SUB // MXU_TILE
    nkb = PEER_N_KEYS // PEER_KEY_BLOCK
    njb = jn // PEER_J_BLOCK

    @pl.when(c == 0)
    def _():
        acc_s[...] = jnp.zeros_like(acc_s)

    nsub = ec // PEER_SUB
    assert ngw == 2

    def scores(sub, gw, buf):
        e0 = pl.multiple_of(sub * PEER_SUB, PEER_SUB)
        res = jnp.dot(u_ref[pl.ds(e0, PEER_SUB), :], xt_ref[gw],
                      preferred_element_type=F32)
        for half in range(halves):
            s_s[buf][half] = res[:, half * LANES:(half + 1) * LANES]

    def outputs(sub, gw, buf):
        rhs = jnp.concatenate([w_s[buf][half] for half in range(halves)], axis=1)
        acc_s[gw] += jnp.dot(vt_ref[sub], rhs, preferred_element_type=F32)

    def gate_block(unit, buf, blk):
        sub, gw = unit
        half = blk // (njb * nkb)
        jj0 = ((blk // nkb) % njb) * PEER_J_BLOCK
        gi = gw * halves + half
        k0 = (blk % nkb) * PEER_KEY_BLOCK
        gate = [None] * PEER_J_BLOCK
        for hd in range(PEER_HEADS):
            s2 = s2_ref[gi, hd, k0:k0 + PEER_KEY_BLOCK, :]
            p2 = p2_ref[gi, hd, k0:k0 + PEER_KEY_BLOCK, :]
            for jo in range(PEER_J_BLOCK):
                j = sub * jn + jj0 + jo
                thr = thr_ref[gi, hd, pl.ds(j, 1), :]
                p1 = p1_ref[gi, hd, pl.ds(j, 1), :]
                term = jnp.where(s2 >= thr, p2, 0.0) * p1
                gate[jo] = term if gate[jo] is None else gate[jo] + term
        for jo in range(PEER_J_BLOCK):
            rows = slice((jj0 + jo) * PEER_N_KEYS + k0, (jj0 + jo) * PEER_N_KEYS + k0 + PEER_KEY_BLOCK)
            act = _twice_gelu_tanh(s_s[buf][half, rows, :])
            w_s[buf][half, rows, :] = (gate[jo] * act).astype(MXU_DT)

    def gates(sub, gw, buf):
        for blk in range(PEER_BLOCKS_PER_UNIT):
            gate_block((sub, gw), buf, blk)

    scores(0, 0, 0)
    w_s[1][...] = jnp.zeros_like(w_s[1])

    def body(sub, carry):
        prev_sub = jnp.maximum(sub - 1, 0)
        next_sub = jnp.minimum(sub + 1, nsub - 1)
        outputs(prev_sub, 1, 1)
        scores(sub, 1, 1)
        gates(sub, 0, 0)
        outputs(sub, 0, 0)
        scores(next_sub, 0, 0)
        gates(sub, 1, 1)
        return carry

    lax.fori_loop(0, nsub, body, 0)
    outputs(nsub - 1, 1, 1)

    @pl.when(c == pl.num_programs(1) - 1)
    def _():
        for gw in range(ngw):
            rows = slice(gw * PEER_WIDE, (gw + 1) * PEER_WIDE)
            y_ref[rows, :] = x_ref[rows, :] + acc_s[gw].T


def _peer_experts(x, xt, thr, p1, s2, p2, u4, vt4, *, tt, ec):
    n, d = x.shape
    ne = u4.shape[0]
    assert n % tt == 0 and tt % PEER_WIDE == 0 and ne % ec == 0 and ec % PEER_SUB == 0
    ng = tt // LANES
    ngw = tt // PEER_WIDE
    jc = ec // PEER_N_KEYS
    halves = PEER_WIDE // LANES
    key_chunk = pl.BlockSpec((ng, PEER_HEADS, jc, LANES), lambda i, c: (i, 0, c, 0))
    all_keys = pl.BlockSpec((ng, PEER_HEADS, PEER_N_KEYS, LANES), lambda i, c: (i, 0, 0, 0))
    return pl.pallas_call(
        functools.partial(_expert_kernel, tt=tt, ec=ec),
        grid=(n // tt, ne // ec),
        in_specs=[
            pl.BlockSpec((tt, d), lambda i, c: (i, 0)),
            pl.BlockSpec((ngw, d, PEER_WIDE), lambda i, c: (i, 0, 0)),
            key_chunk, key_chunk, all_keys, all_keys,
            pl.BlockSpec((ec, d), lambda i, c: (c, 0)),
            pl.BlockSpec((ec // PEER_SUB, d, PEER_SUB), lambda i, c: (c, 0, 0)),
        ],
        out_specs=pl.BlockSpec((tt, d), lambda i, c: (i, 0)),
        out_shape=jax.ShapeDtypeStruct((n, d), F32),
        scratch_shapes=[pltpu.VMEM((ngw, d, PEER_WIDE), F32),
                        pltpu.VMEM((halves, PEER_SUB, LANES), F32), pltpu.VMEM((halves, PEER_SUB, LANES), F32),
                        pltpu.VMEM((halves, PEER_SUB, LANES), MXU_DT),
                        pltpu.VMEM((halves, PEER_SUB, LANES), MXU_DT)],
        compiler_params=pltpu.CompilerParams(dimension_semantics=("arbitrary", "arbitrary"),
                                             vmem_limit_bytes=VMEM_LIMIT),
        name="peer_experts",
    )(x, xt, thr, p1, s2, p2, u4, vt4)


def _peer_weights(wq, k1, k2, u, v):
    ne, d = u.shape
    u4 = u.astype(MXU_DT)
    vt4 = v.astype(MXU_DT).reshape(ne // PEER_SUB, PEER_SUB, d).transpose(0, 2, 1)
    return wq.T.astype(MXU_DT), k1.astype(MXU_DT), k2.astype(MXU_DT), u4, vt4


def _peer_layer(x, g, weights, *, tt_route, tt, ec):
    wqt, k1, k2, u4, vt4 = weights
    shape = x.shape
    x = x.reshape(-1, shape[-1])
    xt, thr, p1, s2, p2 = _peer_route(x, g, wqt, k1, k2, tt=tt_route)
    return _peer_experts(x, xt, thr, p1, s2, p2, u4, vt4, tt=tt, ec=ec).reshape(shape)


def _norm_kernel(x_ref, g_ref, y_ref):
    y_ref[...] = _rms(x_ref[...], g_ref[...])


def _final_norm(x, g, *, tt):
    n, d = x.shape
    return pl.pallas_call(
        _norm_kernel,
        grid=(n // tt,),
        in_specs=[pl.BlockSpec((tt, d), lambda i: (i, 0)), _full((1, d))],
        out_specs=pl.BlockSpec((tt, d), lambda i: (i, 0)),
        out_shape=jax.ShapeDtypeStruct((n, d), F32),
        name="final_norm",
    )(x, g.reshape(1, d))


CONV_PROMPT_TILE = 512
CONV_SAMPLE_SEQS = 8
PEER_ROUTE_TILE = 512
PEER_TOKEN_TILE = 512
PEER_EXPERT_CHUNK = 2048
NORM_TILE = 1024


def kernel(x_prompt, x_sample, cache_conv, state_mlstm_conv, state_C, state_n, state_m, norm_mix, norm_ffn, norm_final, cm_w1, cm_b1, cm_dw, cm_dwb, cm_ln_g, cm_ln_b, cm_w2, cm_b2, ml_wup, ml_convw, ml_convb, ml_wq, ml_wk, ml_wv, ml_wi, ml_bi, ml_wf, ml_bf, ml_norm, ml_skip, ml_wdown, pk_wq, pk_k1, pk_k2, pk_u, pk_v):
    xp, xs = x_prompt, x_sample
    bp, tp, d = xp.shape
    bs, ts, _ = xs.shape
    di, nh, dh = MLSTM_INNER, MLSTM_HEADS, MLSTM_HEAD_DIM
    conv_p, conv_s, mconv_p, mconv_s = [], [], [], []
    np_l, ns_l, mp_l, ms_l = [], [], [], []
    c_p = c_s = None
    for layer in range(DEPTH):
        j = layer // N_MIXERS
        if layer % N_MIXERS == 0:
            cw = (norm_mix[layer], cm_w1[j], cm_b1[j], cm_dw[j], cm_dwb[j], cm_ln_g[j], cm_ln_b[j],
                  cm_w2[j], cm_b2[j])
            xp, hist_p = _conv_layer(xp, jnp.zeros((bp, CONV_HIST, d), F32), *cw, bb=1,
                                     tt=min(tp, CONV_PROMPT_TILE))
            xs, hist_s = _conv_layer(xs, cache_conv[j], *cw, bb=min(bs, CONV_SAMPLE_SEQS), tt=ts)
            conv_p.append(hist_p)
            conv_s.append(hist_s)
        else:
            mw = (norm_mix[layer], ml_wup[j], ml_convw[j], ml_convb[j], ml_wq[j], ml_wk[j], ml_wv[j],
                  ml_wi[j], ml_bi[j], ml_wf[j], ml_bf[j], ml_norm[j], ml_skip[j], ml_wdown[j])
            n_slots = state_C.shape[0]
            xp, hcp, c_p, n_p, m_p = _mlstm_layer(
                xp, jnp.zeros((bp, MLSTM_CONV_HIST, di), F32), jnp.zeros((1, bp, nh, dh, dh), F32), 0,
                c_p, j, n_slots, jnp.zeros((bp, nh, dh), F32), jnp.zeros((bp, nh), F32), *mw,
                chunk=min(tp, MLSTM_PROMPT_CHUNK))
            xs, hcs, c_s, n_s, m_s = _mlstm_layer(
                xs, state_mlstm_conv[j], state_C, j, c_s, j, n_slots, state_n[j], state_m[j], *mw, chunk=ts)
            mconv_p.append(hcp)
            mconv_s.append(hcs)
            np_l.append(n_p)
            ns_l.append(n_s)
            mp_l.append(m_p)
            ms_l.append(m_s)
        pw = _peer_weights(pk_wq[layer], pk_k1[layer], pk_k2[layer], pk_u[layer], pk_v[layer])
        tiles = dict(tt_route=PEER_ROUTE_TILE, tt=PEER_TOKEN_TILE, ec=PEER_EXPERT_CHUNK)
        xp = _peer_layer(xp, norm_ffn[layer], pw, **tiles)
        xs = _peer_layer(xs, norm_ffn[layer], pw, **tiles)
    xp = _final_norm(xp.reshape(bp * tp, d), norm_final, tt=NORM_TILE).reshape(bp, tp, d)
    xs = _final_norm(xs.reshape(bs * ts, d), norm_final, tt=NORM_TILE).reshape(bs, ts, d)
    return (xp, xs,
            jnp.stack(conv_p), jnp.stack(conv_s),
            jnp.stack(mconv_p), jnp.stack(mconv_s),
            c_p, c_s,
            jnp.stack(np_l), jnp.stack(ns_l),
            jnp.stack(mp_l), jnp.stack(ms_l))
```

```python
import functools

import jax
import jax.numpy as jnp
from jax import lax
from jax.experimental import pallas as pl
from jax.experimental.pallas import tpu as pltpu

F32 = jnp.float32
MXU_DT = jnp.bfloat16

D_MODEL = 1024
DEPTH = 4
N_MIXERS = 2
CONV_WIDTH = 31
CONV_HIST = CONV_WIDTH - 1
CONV_PAD = 32
MLSTM_INNER = 2 * D_MODEL
MLSTM_HEADS = 4
MLSTM_HEAD_DIM = MLSTM_INNER // MLSTM_HEADS
MLSTM_QKV_BLOCK = 4
MLSTM_CONV_WIDTH = 4
MLSTM_CONV_HIST = MLSTM_CONV_WIDTH - 1
MLSTM_CONV_PAD = 8
MLSTM_PROMPT_CHUNK = 256
PEER_HEADS = 8
PEER_N_KEYS = 128
PEER_N_EXPERTS = PEER_N_KEYS * PEER_N_KEYS
PEER_TOPK = 16
PEER_HALF = 128
RMS_EPS = 1e-6
LN_EPS = 1e-5

LANES = 128
SUBLANES = 8
MXU_TILE = 256
GATE_ROWS = 16
VMEM_LIMIT = 56 * 1024 * 1024
NEG_INF = float("-inf")
POS_INF = float("inf")


def _full(shape):
    n = len(shape)
    return pl.BlockSpec(shape, lambda *_: (0,) * n)


def _rms(x, g):
    return x * lax.rsqrt(jnp.mean(x * x, axis=-1, keepdims=True) + RMS_EPS) * g


def _mm(a, b):
    return jnp.dot(a.astype(MXU_DT), b.astype(MXU_DT), preferred_element_type=F32)


def _mm_nt(a, b):
    return lax.dot_general(a.astype(MXU_DT), b.astype(MXU_DT), (((1,), (1,)), ((), ())),
                           preferred_element_type=F32)


CONV_ROW_BLOCK = 32


def _conv_kernel(x_ref, hist_ref, g_ref, w1_ref, b1_ref, dw_ref, dwb_ref, lng_ref, lnb_ref, w2_ref, b2_ref,
                 y_ref, nh_ref, ubuf, cbuf, *, bb, tt):
    d = D_MODEL
    t = pl.program_id(1)

    @pl.when(t == 0)
    def _():
        ubuf[:, 0:CONV_PAD, :] = hist_ref[...]

    x = x_ref[...].reshape(bb * tt, d)
    h = _rms(x, g_ref[...])
    a = _mm(h, w1_ref[...]) + b1_ref[...]
    u = a[:, :d] * jax.nn.sigmoid(a[:, d:])
    ubuf[:, CONV_PAD:CONV_PAD + tt, :] = u.reshape(bb, tt, d)

    nrb = tt // CONV_ROW_BLOCK
    lead = CONV_PAD - CONV_HIST

    def row_block(i, carry):
        b = i // nrb
        r0 = pl.multiple_of((i % nrb) * CONV_ROW_BLOCK, CONV_ROW_BLOCK)
        for c in range(d // LANES):
            cs = slice(c * LANES, (c + 1) * LANES)
            win = ubuf[b, pl.ds(r0, CONV_ROW_BLOCK + CONV_PAD), cs]
            wts = dw_ref[:, cs]
            nwin = CONV_ROW_BLOCK + CONV_PAD
            rolled = [win if s == 0 else pltpu.roll(win, nwin - s, 0) for s in range(SUBLANES)]
            acc = jnp.zeros((CONV_ROW_BLOCK, LANES), F32)
            for k in range(CONV_WIDTH):
                s, m = (lead + k) % SUBLANES, (lead + k) // SUBLANES * SUBLANES
                acc = acc + rolled[s][m:m + CONV_ROW_BLOCK, :] * wts[k:k + 1, :]
            cbuf[b, pl.ds(r0, CONV_ROW_BLOCK), cs] = acc
        return carry

    lax.fori_loop(0, bb * nrb, row_block, 0)

    c = cbuf[...].reshape(bb * tt, d) + dwb_ref[...]
    mu = jnp.mean(c, axis=-1, keepdims=True)
    var = jnp.mean(jnp.square(c - mu), axis=-1, keepdims=True)
    c = (c - mu) * lax.rsqrt(var + LN_EPS) * lng_ref[...] + lnb_ref[...]
    c = c * jax.nn.sigmoid(c)
    y = _mm(c, w2_ref[...]) + b2_ref[...]
    y_ref[...] = (x + y).reshape(bb, tt, d)

    new_hist = ubuf[:, tt:tt + CONV_PAD, :]
    nh_ref[...] = new_hist
    ubuf[:, 0:CONV_PAD, :] = new_hist


def _conv_layer(x, hist, g, w1, b1, dw, dwb, lng, lnb, w2, b2, *, bb, tt):
    B, T, d = x.shape
    assert B % bb == 0 and T % tt == 0 and tt % CONV_ROW_BLOCK == 0 and tt >= CONV_PAD
    hist_p = jnp.pad(hist, ((0, 0), (CONV_PAD - CONV_HIST, 0), (0, 0)))
    dw_p = jnp.pad(dw, ((0, CONV_PAD - CONV_WIDTH), (0, 0)))
    row = lambda v: v.reshape(1, -1)
    y, nh = pl.pallas_call(
        functools.partial(_conv_kernel, bb=bb, tt=tt),
        grid=(B // bb, T // tt),
        in_specs=[
            pl.BlockSpec((bb, tt, d), lambda b, t: (b, t, 0)),
            pl.BlockSpec((bb, CONV_PAD, d), lambda b, t: (b, 0, 0)),
            _full((1, d)), _full((d, 2 * d)), _full((1, 2 * d)), _full((CONV_PAD, d)), _full((1, d)),
            _full((1, d)), _full((1, d)), _full((d, d)), _full((1, d)),
        ],
        out_specs=[
            pl.BlockSpec((bb, tt, d), lambda b, t: (b, t, 0)),
            pl.BlockSpec((bb, CONV_PAD, d), lambda b, t: (b, 0, 0)),
        ],
        out_shape=[jax.ShapeDtypeStruct((B, T, d), F32), jax.ShapeDtypeStruct((B, CONV_PAD, d), F32)],
        scratch_shapes=[pltpu.VMEM((bb, CONV_PAD + tt, d), F32), pltpu.VMEM((bb, tt, d), F32)],
        compiler_params=pltpu.CompilerParams(dimension_semantics=("arbitrary", "arbitrary"),
                                             vmem_limit_bytes=VMEM_LIMIT),
        name="conv_mixer",
    )(x, hist_p, row(g), w1.astype(MXU_DT), row(b1), dw_p, row(dwb), row(lng), row(lnb),
      w2.astype(MXU_DT), row(b2))
    return y, nh[:, CONV_PAD - CONV_HIST:, :]


MLSTM_ROW_BLOCK = 32


def _mlstm_kernel(x_ref, hist_ref, c0_ref, n0_ref, m0_ref, g_ref, wup_ref, cw_ref, cb_ref, wq_ref, wk_ref,
                  wv_ref, wg_ref, bg_ref, ng_ref, sk_ref, wd_ref,
                  y_ref, nh_ref, c_ref, n_ref, m_ref, cbuf, xc_s, qkv_s, hh_s, *, L):
    d, di, nh, dh = D_MODEL, MLSTM_INNER, MLSTM_HEADS, MLSTM_HEAD_DIM
    t = pl.program_id(1)

    @pl.when(t == 0)
    def _():
        cbuf[0:MLSTM_CONV_PAD, :] = hist_ref[0]
        c_ref[...] = c0_ref[...]
        n_ref[...] = n0_ref[...]
        m_ref[...] = m0_ref[...]

    x = x_ref[0]
    h = _rms(x, g_ref[...])
    up = _mm(h, wup_ref[...])
    xm = up[:, :di]
    z = up[:, di:]
    cbuf[MLSTM_CONV_PAD:MLSTM_CONV_PAD + L, :] = xm

    lead = MLSTM_CONV_PAD - MLSTM_CONV_HIST

    def row_block(i, carry):
        r0 = pl.multiple_of(i * MLSTM_ROW_BLOCK, MLSTM_ROW_BLOCK)
        for c in range(di // LANES):
            cs = slice(c * LANES, (c + 1) * LANES)
            win = cbuf[pl.ds(r0, MLSTM_ROW_BLOCK + MLSTM_CONV_PAD), cs]
            wts = cw_ref[:, cs]
            acc = jnp.zeros((MLSTM_ROW_BLOCK, LANES), F32) + cb_ref[:, cs]
            for k in range(MLSTM_CONV_WIDTH):
                acc = acc + win[lead + k:lead + k + MLSTM_ROW_BLOCK, :] * wts[k:k + 1, :]
            xc_s[pl.ds(r0, MLSTM_ROW_BLOCK), cs] = acc * jax.nn.sigmoid(acc)
        return carry

    lax.fori_loop(0, L // MLSTM_ROW_BLOCK, row_block, 0)
    new_hist = cbuf[L:L + MLSTM_CONV_PAD, :]
    nh_ref[0] = new_hist
    cbuf[0:MLSTM_CONV_PAD, :] = new_hist

    xc = xc_s[...]
    for c in range(di // MXU_TILE):
        cs = slice(c * MXU_TILE, (c + 1) * MXU_TILE)
        qkv_s[:, c * MXU_TILE:(c + 1) * MXU_TILE] = _mm(xc[:, cs], wq_ref[c])
        qkv_s[:, di + c * MXU_TILE:di + (c + 1) * MXU_TILE] = _mm(xc[:, cs], wk_ref[c])
        qkv_s[:, 2 * di + c * MXU_TILE:2 * di + (c + 1) * MXU_TILE] = _mm(xm[:, cs], wv_ref[c])

    gates = _mm_nt(wg_ref[...], qkv_s[...]) + bg_ref[...]
    ig_all = gates[0:nh, :]
    fp = gates[0:SUBLANES, :]
    lf_all = jnp.minimum(fp, 0.0) - jnp.log(1.0 + jnp.exp(-jnp.abs(fp)))
    ri = lax.broadcasted_iota(jnp.int32, (L, L), 0)
    ci = lax.broadcasted_iota(jnp.int32, (L, L), 1)
    causal = ri >= ci
    eye = ri == ci
    tri = jnp.where(ri <= ci, 1.0, 0.0).astype(F32)
    b_all = jnp.dot(lf_all, tri, preferred_element_type=F32, precision=lax.Precision.HIGHEST)

    def to_col(row):
        return jnp.sum(jnp.where(eye, row, 0.0), axis=1, keepdims=True)

    kscale = dh ** -0.5
    for hd in range(nh):
        hs = slice(hd * dh, (hd + 1) * dh)
        qh = qkv_s[:, hd * dh:(hd + 1) * dh]
        kh = qkv_s[:, di + hd * dh:di + (hd + 1) * dh] * kscale
        vh = qkv_s[:, 2 * di + hd * dh:2 * di + (hd + 1) * dh]
        ig_r = ig_all[hd:hd + 1, :]
        b_r = b_all[nh + hd:nh + hd + 1, :]
        b_c = to_col(b_r)
        m_prev = m_ref[0, hd:hd + 1, 0:1]
        n_r = n_ref[0, hd:hd + 1, :]
        cmat = c_ref[0, 0, hd]

        logw = jnp.where(causal, b_c - b_r + ig_r, NEG_INF)
        inter = b_c + m_prev
        m_row = jnp.maximum(jnp.max(logw, axis=1, keepdims=True), inter)
        w_intra = jnp.exp(logw - m_row)
        w_inter = jnp.exp(inter - m_row)
        s = _mm_nt(qh, kh) * w_intra
        num = _mm(s, vh) + w_inter * _mm(qh, cmat)
        den = jnp.sum(s, axis=1, keepdims=True) + w_inter * jnp.sum(qh * n_r, axis=1, keepdims=True)
        hh = num / jnp.maximum(jnp.abs(den), jnp.exp(-m_row))

        mu = jnp.mean(hh, axis=-1, keepdims=True)
        var = jnp.mean(jnp.square(hh - mu), axis=-1, keepdims=True)
        hh_s[:, hs] = (hh - mu) * lax.rsqrt(var + LN_EPS)

        b_end = b_r[:, L - 1:L]
        g_r = b_end - b_r + ig_r
        m_new = jnp.maximum(b_end + m_prev, jnp.max(g_r, axis=1, keepdims=True))
        wk_c = to_col(jnp.exp(g_r - m_new))
        decay = jnp.exp(b_end + m_prev - m_new)
        kw = kh * wk_c
        upd = lax.dot_general(kw.astype(MXU_DT), vh.astype(MXU_DT), (((0,), (0,)), ((), ())),
                              preferred_element_type=F32)
        c_ref[0, 0, hd] = decay * cmat + upd
        n_ref[0, hd:hd + 1, :] = decay * n_r + jnp.sum(kw, axis=0, keepdims=True)
        m_ref[0, hd:hd + 1, :] = jnp.broadcast_to(m_new, (1, LANES))

    hn = hh_s[...] * ng_ref[...]
    out = jax.nn.sigmoid(z) * (hn + sk_ref[...] * xc)
    y_ref[0] = x + _mm(out, wd_ref[...])


def _blockdiag_tiles(w):
    g, bs, _ = w.shape
    per = MXU_TILE // bs
    wt = w.reshape(g // per, per, bs, bs)
    eye = jnp.eye(per, dtype=w.dtype)
    dense = jnp.einsum("tpij,pq->tpiqj", wt, eye)
    return dense.reshape(g // per, MXU_TILE, MXU_TILE)


def _mlstm_kernel_into(c_all_ref, *refs, L):
    del c_all_ref
    _mlstm_kernel(*refs, L=L)


def _mlstm_layer(x, hist, c0, slot_in, c_out, slot_out, n_slots, n0, m0,
                 g, wup, cw, cb, wq, wk, wv, wi, bi, wf, bf, ng, sk, wd, *, chunk):
    B, T, d = x.shape
    di, nh, dh, L = MLSTM_INNER, MLSTM_HEADS, MLSTM_HEAD_DIM, chunk
    assert T % L == 0 and L % MLSTM_ROW_BLOCK == 0
    hist_p = jnp.pad(hist, ((0, 0), (MLSTM_CONV_PAD - MLSTM_CONV_HIST, 0), (0, 0)))
    cw_p = jnp.pad(cw, ((0, SUBLANES - MLSTM_CONV_WIDTH), (0, 0)))
    n0_p = jnp.pad(n0, ((0, 0), (0, SUBLANES - nh), (0, 0)))
    m0_p = jnp.broadcast_to(jnp.pad(m0, ((0, 0), (0, SUBLANES - nh)))[:, :, None], (B, SUBLANES, LANES))
    wg = jnp.pad(jnp.concatenate([wi, wf], axis=1).T, ((0, GATE_ROWS - 2 * nh), (0, 0)))
    bg = jnp.pad(jnp.concatenate([bi, bf]), (0, GATE_ROWS - 2 * nh)).reshape(GATE_ROWS, 1)
    row = lambda v: v.reshape(1, -1)
    nt = di // MXU_TILE
    into = c_out is not None
    y, nhist, c_new, n_new, m_new = pl.pallas_call(
        functools.partial(_mlstm_kernel_into if into else _mlstm_kernel, L=L),
        grid=(B, T // L),
        in_specs=([pl.BlockSpec(memory_space=pl.ANY)] if into else []) + [
            pl.BlockSpec((1, L, d), lambda b, t: (b, t, 0)),
            pl.BlockSpec((1, MLSTM_CONV_PAD, di), lambda b, t: (b, 0, 0)),
            pl.BlockSpec((1, 1, nh, dh, dh), lambda b, t: (slot_in, b, 0, 0, 0)),
            pl.BlockSpec((1, SUBLANES, dh), lambda b, t: (b, 0, 0)),
            pl.BlockSpec((1, SUBLANES, LANES), lambda b, t: (b, 0, 0)),
            _full((1, d)), _full((d, 2 * di)), _full((SUBLANES, di)), _full((1, di)),
            _full((nt, MXU_TILE, MXU_TILE)), _full((nt, MXU_TILE, MXU_TILE)), _full((nt, MXU_TILE, MXU_TILE)),
            _full((GATE_ROWS, 3 * di)), _full((GATE_ROWS, 1)), _full((1, di)), _full((1, di)), _full((di, d)),
        ],
        out_specs=[
            pl.BlockSpec((1, L, d), lambda b, t: (b, t, 0)),
            pl.BlockSpec((1, MLSTM_CONV_PAD, di), lambda b, t: (b, 0, 0)),
            pl.BlockSpec((1, 1, nh, dh, dh), lambda b, t: (slot_out, b, 0, 0, 0)),
            pl.BlockSpec((1, SUBLANES, dh), lambda b, t: (b, 0, 0)),
            pl.BlockSpec((1, SUBLANES, LANES), lambda b, t: (b, 0, 0)),
        ],
        out_shape=[
            jax.ShapeDtypeStruct((B, T, d), F32),
            jax.ShapeDtypeStruct((B, MLSTM_CONV_PAD, di), F32),
            jax.ShapeDtypeStruct((n_slots, B, nh, dh, dh), F32),
            jax.ShapeDtypeStruct((B, SUBLANES, dh), F32),
            jax.ShapeDtypeStruct((B, SUBLANES, LANES), F32),
        ],
        scratch_shapes=[
            pltpu.VMEM((MLSTM_CONV_PAD + L, di), F32),
            pltpu.VMEM((L, di), F32),
            pltpu.VMEM((L, 3 * di), F32),
            pltpu.VMEM((L, di), F32),
        ],
        compiler_params=pltpu.CompilerParams(dimension_semantics=("arbitrary", "arbitrary"),
                                             vmem_limit_bytes=VMEM_LIMIT),
        input_output_aliases={0: 2} if into else {},
        name="mlstm_mixer",
    )(*((c_out,) if into else ()), x, hist_p, c0, n0_p, m0_p, row(g), wup.astype(MXU_DT), cw_p, row(cb),
      _blockdiag_tiles(wq).astype(MXU_DT), _blockdiag_tiles(wk).astype(MXU_DT),
      _blockdiag_tiles(wv).astype(MXU_DT), wg.astype(MXU_DT), bg, row(ng), row(sk), wd.astype(MXU_DT))
    return (y, nhist[:, MLSTM_CONV_PAD - MLSTM_CONV_HIST:, :], c_new, n_new[:, :nh, :], m_new[:, :nh, 0])


def _sorting_network(n):
    pairs = []
    p = 1
    while p < n:
        k = p
        while k >= 1:
            for j in range(k % p, n - k, 2 * k):
                for i in range(min(k, n - j - k)):
                    if (i + j) // (2 * p) == (i + j + k) // (2 * p):
                        pairs.append((i + j, i + j + k))
            k //= 2
        p *= 2
    return pairs


def _top16(tile):
    nrow = PEER_N_KEYS // SUBLANES
    c = [tile[r * SUBLANES:(r + 1) * SUBLANES, :] for r in range(nrow)]
    for i, j in _sorting_network(nrow):
        c[i], c[j] = jnp.maximum(c[i], c[j]), jnp.minimum(c[i], c[j])
    vals = []
    for a in range(PEER_TOPK):
        mx = jnp.max(c[0], axis=0, keepdims=True)
        vals.append(mx)
        if a + 1 < PEER_TOPK:
            hit = c[0] == mx
            for i in range(PEER_TOPK - 1 - a):
                c[i] = jnp.where(hit, c[i + 1], c[i])
    return vals


def _pack_heads(rows):
    sub = lax.broadcasted_iota(jnp.int32, (PEER_HEADS, LANES), 0)
    out = jnp.broadcast_to(rows[0], (PEER_HEADS, LANES))
    for hd in range(1, PEER_HEADS):
        out = jnp.where(sub == hd, rows[hd], out)
    return out


def _route_kernel(x_ref, g_ref, wqt_ref, k1_ref, k2_ref,
                  xt_ref, thr_ref, p1_ref, s2_ref, p2_ref, s1_s, *, tt):
    ng = tt // LANES
    xn = _rms(x_ref[...], g_ref[...])
    xt = xn.T.astype(MXU_DT)
    for gw in range(tt // PEER_WIDE):
        xt_ref[gw] = xt[:, gw * PEER_WIDE:(gw + 1) * PEER_WIDE]
    qt = jnp.dot(wqt_ref[...], xt, preferred_element_type=F32)
    for hd in range(PEER_HEADS):
        q1 = qt[hd * 2 * PEER_HALF:hd * 2 * PEER_HALF + PEER_HALF, :]
        q2 = qt[hd * 2 * PEER_HALF + PEER_HALF:(hd + 1) * 2 * PEER_HALF, :]
        s1 = _mm(k1_ref[hd], q1)
        s2 = _mm(k2_ref[hd], q2)
        for gi in range(ng):
            s1_s[gi, hd] = s1[:, gi * LANES:(gi + 1) * LANES]
            s2_ref[gi, hd] = s2[:, gi * LANES:(gi + 1) * LANES]

    pairs = [(a, b) for a in range(PEER_TOPK) for b in range(PEER_TOPK) if (a + 1) * (b + 1) <= PEER_TOPK]

    def group(gi, carry):
        v1 = [_top16(s1_s[gi, hd]) for hd in range(PEER_HEADS)]
        v2 = [_top16(s2_ref[gi, hd]) for hd in range(PEER_HEADS)]
        V1 = [_pack_heads([v1[hd][a] for hd in range(PEER_HEADS)]) for a in range(PEER_TOPK)]
        V2 = [_pack_heads([v2[hd][a] for hd in range(PEER_HEADS)]) for a in range(PEER_TOPK)]
        cand = [V1[a] + V2[b] for (a, b) in pairs]
        cur = list(cand)
        tau = None
        for it in range(PEER_TOPK):
            mx = cur[0]
            for cnd in cur[1:]:
                mx = jnp.maximum(mx, cnd)
            if it + 1 < PEER_TOPK:
                cur = [jnp.where(cnd == mx, NEG_INF, cnd) for cnd in cur]
            else:
                tau = mx
        E1 = [jnp.exp(V1[a] - V1[0]) for a in range(PEER_TOPK)]
        E2 = [jnp.exp(V2[b] - V2[0]) for b in range(PEER_TOPK)]
        zsum = jnp.zeros((PEER_HEADS, LANES), F32)
        for idx, (a, b) in enumerate(pairs):
            zsum = zsum + jnp.where(cand[idx] >= tau, E1[a] * E2[b], 0.0)
        inv_z = 0.5 / zsum
        T = [None] * PEER_TOPK
        for idx, (a, b) in enumerate(pairs):
            cut = jnp.where(cand[idx] >= tau, V2[b], POS_INF)
            T[a] = cut if T[a] is None else jnp.minimum(T[a], cut)
        for hd in range(PEER_HEADS):
            s1 = s1_s[gi, hd]
            thr = jnp.full((PEER_N_KEYS, LANES), POS_INF, F32)
            for a in range(PEER_TOPK):
                thr = jnp.minimum(thr, jnp.where(s1 >= v1[hd][a], T[a][hd:hd + 1, :], POS_INF))
            thr_ref[gi, hd] = thr
            p1_ref[gi, hd] = jnp.exp(s1 - v1[hd][0]) * inv_z[hd:hd + 1, :]
            p2_ref[gi, hd] = jnp.exp(s2_ref[gi, hd] - v2[hd][0])
        return carry

    lax.fori_loop(0, ng, group, 0)


def _peer_route(x, g, wqt, k1, k2, *, tt):
    n, d = x.shape
    assert n % tt == 0 and tt % PEER_WIDE == 0
    ng = tt // LANES
    tile_spec = pl.BlockSpec((ng, PEER_HEADS, PEER_N_KEYS, LANES), lambda i: (i, 0, 0, 0))
    tile_shape = jax.ShapeDtypeStruct((n // LANES, PEER_HEADS, PEER_N_KEYS, LANES), F32)
    return pl.pallas_call(
        functools.partial(_route_kernel, tt=tt),
        grid=(n // tt,),
        in_specs=[
            pl.BlockSpec((tt, d), lambda i: (i, 0)),
            _full((1, d)), _full((2 * PEER_HALF * PEER_HEADS, d)),
            _full((PEER_HEADS, PEER_N_KEYS, PEER_HALF)), _full((PEER_HEADS, PEER_N_KEYS, PEER_HALF)),
        ],
        out_specs=[pl.BlockSpec((tt // PEER_WIDE, d, PEER_WIDE), lambda i: (i, 0, 0)),
                   tile_spec, tile_spec, tile_spec, tile_spec],
        out_shape=[jax.ShapeDtypeStruct((n // PEER_WIDE, d, PEER_WIDE), MXU_DT),
                   tile_shape, tile_shape, tile_shape, tile_shape],
        scratch_shapes=[pltpu.VMEM((ng, PEER_HEADS, PEER_N_KEYS, LANES), F32)],
        compiler_params=pltpu.CompilerParams(dimension_semantics=("arbitrary",),
                                             vmem_limit_bytes=VMEM_LIMIT),
        name="peer_route",
    )(x, g.reshape(1, d), wqt, k1, k2)


PEER_SUB = 512
PEER_WIDE = MXU_TILE
PEER_KEY_BLOCK = 32
PEER_J_BLOCK = 4
PEER_PIECES = D_MODEL // MXU_TILE
PEER_OUT_ROWS = 512
PEER_BLOCKS_PER_UNIT = ((PEER_WIDE // LANES) * (PEER_SUB // PEER_N_KEYS // PEER_J_BLOCK)
                        * (PEER_N_KEYS // PEER_KEY_BLOCK))


GELU_C0 = 0.7978845608028654
GELU_C1 = 0.044715 * GELU_C0


def _twice_gelu_tanh(x):
    return x * (1.0 + jnp.tanh(x * (GELU_C0 + GELU_C1 * (x * x))))


def _expert_kernel(x_ref, xt_ref, thr_ref, p1_ref, s2_ref, p2_ref, u_ref, vt_ref, y_ref,
                   acc_s, s_a, s_b, w_a, w_b, *, tt, ec):
    s_s = (s_a, s_b)
    w_s = (w_a, w_b)
    c = pl.program_id(1)
    ngw = tt // PEER_WIDE
    halves = PEER_WIDE // LANES
    jn = PEER_SUB // PEER_N_KEYS
    kts = PEER_SUB // MXU_TILE
    nkb = PEER_N_KEYS // PEER_KEY_BLOCK
    njb = jn // PEER_J_BLOCK

    @pl.when(c == 0)
    def _():
        acc_s[...] = jnp.zeros_like(acc_s)

    nsub = ec // PEER_SUB
    assert ngw == 2

    def scores(sub, gw, buf):
        e0 = pl.multiple_of(sub * PEER_SUB, PEER_SUB)
        res = jnp.dot(u_ref[pl.ds(e0, PEER_SUB), :], xt_ref[gw],
                      preferred_element_type=F32)
        for half in range(halves):
            s_s[buf][half] = res[:, half * LANES:(half + 1) * LANES]

    def outputs(sub, gw, buf):
        rhs = jnp.concatenate([w_s[buf][half] for half in range(halves)], axis=1)
        acc_s[gw] += jnp.dot(vt_ref[sub], rhs, preferred_element_type=F32)

    def gate_block(unit, buf, blk):
        sub, gw = unit
        half = blk // (njb * nkb)
        jj0 = ((blk // nkb) % njb) * PEER_J_BLOCK
        gi = gw * halves + half
        k0 = (blk % nkb) * PEER_KEY_BLOCK
        gate = [None] * PEER_J_BLOCK
        for hd in range(PEER_HEADS):
            s2 = s2_ref[gi, hd, k0:k0 + PEER_KEY_BLOCK, :]
            p2 = p2_ref[gi, hd, k0:k0 + PEER_KEY_BLOCK, :]
            for jo in range(PEER_J_BLOCK):
                j = sub * jn + jj0 + jo
                thr = thr_ref[gi, hd, pl.ds(j, 1), :]
                p1 = p1_ref[gi, hd, pl.ds(j, 1), :]
                term = jnp.where(s2 >= thr, p2, 0.0) * p1
                gate[jo] = term if gate[jo] is None else gate[jo] + term
        for jo in range(PEER_J_BLOCK):
            rows = slice((jj0 + jo) * PEER_N_KEYS + k0, (jj0 + jo) * PEER_N_KEYS + k0 + PEER_KEY_BLOCK)
            act = _twice_gelu_tanh(s_s[buf][half, rows, :])
            w_s[buf][half, rows, :] = (gate[jo] * act).astype(MXU_DT)

    def gates(sub, gw, buf):
        for blk in range(PEER_BLOCKS_PER_UNIT):
            gate_block((sub, gw), buf, blk)

    scores(0, 0, 0)
    w_s[1][...] = jnp.zeros_like(w_s[1])

    def body(sub, carry):
        prev_sub = jnp.maximum(sub - 1, 0)
        next_sub = jnp.minimum(sub + 1, nsub - 1)
        outputs(prev_sub, 1, 1)
        scores(sub, 1, 1)
        gates(sub, 0, 0)
        outputs(sub, 0, 0)
        scores(next_sub, 0, 0)
        gates(sub, 1, 1)
        return carry

    lax.fori_loop(0, nsub, body, 0)
    outputs(nsub - 1, 1, 1)

    @pl.when(c == pl.num_programs(1) - 1)
    def _():
        for gw in range(ngw):
            rows = slice(gw * PEER_WIDE, (gw + 1) * PEER_WIDE)
            y_ref[rows, :] = x_ref[rows, :] + acc_s[gw].T


def _peer_experts(x, xt, thr, p1, s2, p2, u4, vt4, *, tt, ec):
    n, d = x.shape
    ne = u4.shape[0]
    assert n % tt == 0 and tt % PEER_WIDE == 0 and ne % ec == 0 and ec % PEER_SUB == 0
    ng = tt // LANES
    ngw = tt // PEER_WIDE
    jc = ec // PEER_N_KEYS
    halves = PEER_WIDE // LANES
    key_chunk = pl.BlockSpec((ng, PEER_HEADS, jc, LANES), lambda i, c: (i, 0, c, 0))
    all_keys = pl.BlockSpec((ng, PEER_HEADS, PEER_N_KEYS, LANES), lambda i, c: (i, 0, 0, 0))
    return pl.pallas_call(
        functools.partial(_expert_kernel, tt=tt, ec=ec),
        grid=(n // tt, ne // ec),
        in_specs=[
            pl.BlockSpec((tt, d), lambda i, c: (i, 0)),
            pl.BlockSpec((ngw, d, PEER_WIDE), lambda i, c: (i, 0, 0)),
            key_chunk, key_chunk, all_keys, all_keys,
            pl.BlockSpec((ec, d), lambda i, c: (c, 0)),
            pl.BlockSpec((ec // PEER_SUB, d, PEER_SUB), lambda i, c: (c, 0, 0)),
        ],
        out_specs=pl.BlockSpec((tt, d), lambda i, c: (i, 0)),
        out_shape=jax.ShapeDtypeStruct((n, d), F32),
        scratch_shapes=[pltpu.VMEM((ngw, d, PEER_WIDE), F32),
                        pltpu.VMEM((halves, PEER_SUB, LANES), F32), pltpu.VMEM((halves, PEER_SUB, LANES), F32),
                        pltpu.VMEM((halves, PEER_SUB, LANES), MXU_DT),
                        pltpu.VMEM((halves, PEER_SUB, LANES), MXU_DT)],
        compiler_params=pltpu.CompilerParams(dimension_semantics=("arbitrary", "arbitrary"),
                                             vmem_limit_bytes=VMEM_LIMIT),
        name="peer_experts",
    )(x, xt, thr, p1, s2, p2, u4, vt4)


def _peer_weights(wq, k1, k2, u, v):
    ne, d = u.shape
    u4 = u.astype(MXU_DT)
    vt4 = v.astype(MXU_DT).reshape(ne // PEER_SUB, PEER_SUB, d).transpose(0, 2, 1)
    return wq.T.astype(MXU_DT), k1.astype(MXU_DT), k2.astype(MXU_DT), u4, vt4


def _peer_layer(x, g, weights, *, tt_route, tt, ec):
    wqt, k1, k2, u4, vt4 = weights
    shape = x.shape
    x = x.reshape(-1, shape[-1])
    xt, thr, p1, s2, p2 = _peer_route(x, g, wqt, k1, k2, tt=tt_route)
    return _peer_experts(x, xt, thr, p1, s2, p2, u4, vt4, tt=tt, ec=ec).reshape(shape)


def _norm_kernel(x_ref, g_ref, y_ref):
    y_ref[...] = _rms(x_ref[...], g_ref[...])


def _final_norm(x, g, *, tt):
    n, d = x.shape
    return pl.pallas_call(
        _norm_kernel,
        grid=(n // tt,),
        in_specs=[pl.BlockSpec((tt, d), lambda i: (i, 0)), _full((1, d))],
        out_specs=pl.BlockSpec((tt, d), lambda i: (i, 0)),
        out_shape=jax.ShapeDtypeStruct((n, d), F32),
        name="final_norm",
    )(x, g.reshape(1, d))


CONV_PROMPT_TILE = 512
CONV_SAMPLE_SEQS = 8
PEER_ROUTE_TILE = 512
PEER_TOKEN_TILE = 512
PEER_EXPERT_CHUNK = 2048
NORM_TILE = 1024


def kernel(x_prompt, x_sample, cache_conv, state_mlstm_conv, state_C, state_n, state_m, norm_mix, norm_ffn, norm_final, cm_w1, cm_b1, cm_dw, cm_dwb, cm_ln_g, cm_ln_b, cm_w2, cm_b2, ml_wup, ml_convw, ml_convb, ml_wq, ml_wk, ml_wv, ml_wi, ml_bi, ml_wf, ml_bf, ml_norm, ml_skip, ml_wdown, pk_wq, pk_k1, pk_k2, pk_u, pk_v):
    xp, xs = x_prompt, x_sample
    bp, tp, d = xp.shape
    bs, ts, _ = xs.shape
    di, nh, dh = MLSTM_INNER, MLSTM_HEADS, MLSTM_HEAD_DIM
    conv_p, conv_s, mconv_p, mconv_s = [], [], [], []
    np_l, ns_l, mp_l, ms_l = [], [], [], []
    c_p = c_s = None
    for layer in range(DEPTH):
        j = layer // N_MIXERS
        if layer % N_MIXERS == 0:
            cw = (norm_mix[layer], cm_w1[j], cm_b1[j], cm_dw[j], cm_dwb[j], cm_ln_g[j], cm_ln_b[j],
                  cm_w2[j], cm_b2[j])
            xp, hist_p = _conv_layer(xp, jnp.zeros((bp, CONV_HIST, d), F32), *cw, bb=1,
                                     tt=min(tp, CONV_PROMPT_TILE))
            xs, hist_s = _conv_layer(xs, cache_conv[j], *cw, bb=min(bs, CONV_SAMPLE_SEQS), tt=ts)
            conv_p.append(hist_p)
            conv_s.append(hist_s)
        else:
            mw = (norm_mix[layer], ml_wup[j], ml_convw[j], ml_convb[j], ml_wq[j], ml_wk[j], ml_wv[j],
                  ml_wi[j], ml_bi[j], ml_wf[j], ml_bf[j], ml_norm[j], ml_skip[j], ml_wdown[j])
            n_slots = state_C.shape[0]
            xp, hcp, c_p, n_p, m_p = _mlstm_layer(
                xp, jnp.zeros((bp, MLSTM_CONV_HIST, di), F32), jnp.zeros((1, bp, nh, dh, dh), F32), 0,
                c_p, j, n_slots, jnp.zeros((bp, nh, dh), F32), jnp.zeros((bp, nh), F32), *mw,
                chunk=min(tp, MLSTM_PROMPT_CHUNK))
            xs, hcs, c_s, n_s, m_s = _mlstm_layer(
                xs, state_mlstm_conv[j], state_C, j, c_s, j, n_slots, state_n[j], state_m[j], *mw, chunk=ts)
            mconv_p.append(hcp)
            mconv_s.append(hcs)
            np_l.append(n_p)
            ns_l.append(n_s)
            mp_l.append(m_p)
            ms_l.append(m_s)
        pw = _peer_weights(pk_wq[layer], pk_k1[layer], pk_k2[layer], pk_u[layer], pk_v[layer])
        tiles = dict(tt_route=PEER_ROUTE_TILE, tt=PEER_TOKEN_TILE, ec=PEER_EXPERT_CHUNK)
        xp = _peer_layer(xp, norm_ffn[layer], pw, **tiles)
        xs = _peer_layer(xs, norm_ffn[layer], pw, **tiles)
    xp = _final_norm(xp.reshape(bp * tp, d), norm_final, tt=NORM_TILE).reshape(bp, tp, d)
    xs = _final_norm(xs.reshape(bs * ts, d), norm_final, tt=NORM_TILE).reshape(bs, ts, d)
    return (xp, xs,
            jnp.stack(conv_p), jnp.stack(conv_s),
            jnp.stack(mconv_p), jnp.stack(mconv_s),
            c_p, c_s,
            jnp.stack(np_l), jnp.stack(ns_l),
            jnp.stack(mp_l), jnp.stack(ms_l))
```

```python
import functools

import jax
import jax.numpy as jnp
from jax import lax
from jax.experimental import pallas as pl
from jax.experimental.pallas import tpu as pltpu

F32 = jnp.float32
MXU_DT = jnp.bfloat16

D_MODEL = 1024
DEPTH = 4
N_MIXERS = 2
CONV_WIDTH = 31
CONV_HIST = CONV_WIDTH - 1
CONV_PAD = 32
MLSTM_INNER = 2 * D_MODEL
MLSTM_HEADS = 4
MLSTM_HEAD_DIM = MLSTM_INNER // MLSTM_HEADS
MLSTM_QKV_BLOCK = 4
MLSTM_CONV_WIDTH = 4
MLSTM_CONV_HIST = MLSTM_CONV_WIDTH - 1
MLSTM_CONV_PAD = 8
MLSTM_PROMPT_CHUNK = 256
PEER_HEADS = 8
PEER_N_KEYS = 128
PEER_N_EXPERTS = PEER_N_KEYS * PEER_N_KEYS
PEER_TOPK = 16
PEER_HALF = 128
RMS_EPS = 1e-6
LN_EPS = 1e-5

LANES = 128
SUBLANES = 8
MXU_TILE = 256
GATE_ROWS = 16
VMEM_LIMIT = 56 * 1024 * 1024
NEG_INF = float("-inf")
POS_INF = float("inf")


def _full(shape):
    n = len(shape)
    return pl.BlockSpec(shape, lambda *_: (0,) * n)


def _rms(x, g):
    return x * lax.rsqrt(jnp.mean(x * x, axis=-1, keepdims=True) + RMS_EPS) * g


def _mm(a, b):
    return jnp.dot(a.astype(MXU_DT), b.astype(MXU_DT), preferred_element_type=F32)


def _mm_nt(a, b):
    return lax.dot_general(a.astype(MXU_DT), b.astype(MXU_DT), (((1,), (1,)), ((), ())),
                           preferred_element_type=F32)


CONV_ROW_BLOCK = 32


def _conv_kernel(x_ref, hist_ref, g_ref, w1_ref, b1_ref, dw_ref, dwb_ref, lng_ref, lnb_ref, w2_ref, b2_ref,
                 y_ref, nh_ref, ubuf, cbuf, *, bb, tt):
    d = D_MODEL
    t = pl.program_id(1)

    @pl.when(t == 0)
    def _():
        ubuf[:, 0:CONV_PAD, :] = hist_ref[...]

    x = x_ref[...].reshape(bb * tt, d)
    h = _rms(x, g_ref[...])
    a = _mm(h, w1_ref[...]) + b1_ref[...]
    u = a[:, :d] * jax.nn.sigmoid(a[:, d:])
    ubuf[:, CONV_PAD:CONV_PAD + tt, :] = u.reshape(bb, tt, d)

    nrb = tt // CONV_ROW_BLOCK
    lead = CONV_PAD - CONV_HIST

    def row_block(i, carry):
        b = i // nrb
        r0 = pl.multiple_of((i % nrb) * CONV_ROW_BLOCK, CONV_ROW_BLOCK)
        for c in range(d // LANES):
            cs = slice(c * LANES, (c + 1) * LANES)
            win = ubuf[b, pl.ds(r0, CONV_ROW_BLOCK + CONV_PAD), cs]
            wts = dw_ref[:, cs]
            nwin = CONV_ROW_BLOCK + CONV_PAD
            rolled = [win if s == 0 else pltpu.roll(win, nwin - s, 0) for s in range(SUBLANES)]
            acc = jnp.zeros((CONV_ROW_BLOCK, LANES), F32)
            for k in range(CONV_WIDTH):
                s, m = (lead + k) % SUBLANES, (lead + k) // SUBLANES * SUBLANES
                acc = acc + rolled[s][m:m + CONV_ROW_BLOCK, :] * wts[k:k + 1, :]
            cbuf[b, pl.ds(r0, CONV_ROW_BLOCK), cs] = acc
        return carry

    lax.fori_loop(0, bb * nrb, row_block, 0)

    c = cbuf[...].reshape(bb * tt, d) + dwb_ref[...]
    mu = jnp.mean(c, axis=-1, keepdims=True)
    var = jnp.mean(jnp.square(c - mu), axis=-1, keepdims=True)
    c = (c - mu) * lax.rsqrt(var + LN_EPS) * lng_ref[...] + lnb_ref[...]
    c = c * jax.nn.sigmoid(c)
    y = _mm(c, w2_ref[...]) + b2_ref[...]
    y_ref[...] = (x + y).reshape(bb, tt, d)

    new_hist = ubuf[:, tt:tt + CONV_PAD, :]
    nh_ref[...] = new_hist
    ubuf[:, 0:CONV_PAD, :] = new_hist


def _conv_layer(x, hist, g, w1, b1, dw, dwb, lng, lnb, w2, b2, *, bb, tt):
    B, T, d = x.shape
    assert B % bb == 0 and T % tt == 0 and tt % CONV_ROW_BLOCK == 0 and tt >= CONV_PAD
    hist_p = jnp.pad(hist, ((0, 0), (CONV_PAD - CONV_HIST, 0), (0, 0)))
    dw_p = jnp.pad(dw, ((0, CONV_PAD - CONV_WIDTH), (0, 0)))
    row = lambda v: v.reshape(1, -1)
    y, nh = pl.pallas_call(
        functools.partial(_conv_kernel, bb=bb, tt=tt),
        grid=(B // bb, T // tt),
        in_specs=[
            pl.BlockSpec((bb, tt, d), lambda b, t: (b, t, 0)),
            pl.BlockSpec((bb, CONV_PAD, d), lambda b, t: (b, 0, 0)),
            _full((1, d)), _full((d, 2 * d)), _full((1, 2 * d)), _full((CONV_PAD, d)), _full((1, d)),
            _full((1, d)), _full((1, d)), _full((d, d)), _full((1, d)),
        ],
        out_specs=[
            pl.BlockSpec((bb, tt, d), lambda b, t: (b, t, 0)),
            pl.BlockSpec((bb, CONV_PAD, d), lambda b, t: (b, 0, 0)),
        ],
        out_shape=[jax.ShapeDtypeStruct((B, T, d), F32), jax.ShapeDtypeStruct((B, CONV_PAD, d), F32)],
        scratch_shapes=[pltpu.VMEM((bb, CONV_PAD + tt, d), F32), pltpu.VMEM((bb, tt, d), F32)],
        compiler_params=pltpu.CompilerParams(dimension_semantics=("arbitrary", "arbitrary"),
                                             vmem_limit_bytes=VMEM_LIMIT),
        name="conv_mixer",
    )(x, hist_p, row(g), w1.astype(MXU_DT), row(b1), dw_p, row(dwb), row(lng), row(lnb),
      w2.astype(MXU_DT), row(b2))
    return y, nh[:, CONV_PAD - CONV_HIST:, :]


MLSTM_ROW_BLOCK = 32


def _mlstm_kernel(x_ref, hist_ref, c0_ref, n0_ref, m0_ref, g_ref, wup_ref, cw_ref, cb_ref, wq_ref, wk_ref,
                  wv_ref, wg_ref, bg_ref, ng_ref, sk_ref, wd_ref,
                  y_ref, nh_ref, c_ref, n_ref, m_ref, cbuf, xc_s, qkv_s, hh_s, *, L):
    d, di, nh, dh = D_MODEL, MLSTM_INNER, MLSTM_HEADS, MLSTM_HEAD_DIM
    t = pl.program_id(1)

    @pl.when(t == 0)
    def _():
        cbuf[0:MLSTM_CONV_PAD, :] = hist_ref[0]
        c_ref[...] = c0_ref[...]
        n_ref[...] = n0_ref[...]
        m_ref[...] = m0_ref[...]

    x = x_ref[0]
    h = _rms(x, g_ref[...])
    up = _mm(h, wup_ref[...])
    xm = up[:, :di]
    z = up[:, di:]
    cbuf[MLSTM_CONV_PAD:MLSTM_CONV_PAD + L, :] = xm

    lead = MLSTM_CONV_PAD - MLSTM_CONV_HIST

    def row_block(i, carry):
        r0 = pl.multiple_of(i * MLSTM_ROW_BLOCK, MLSTM_ROW_BLOCK)
        for c in range(di // LANES):
            cs = slice(c * LANES, (c + 1) * LANES)
            win = cbuf[pl.ds(r0, MLSTM_ROW_BLOCK + MLSTM_CONV_PAD), cs]
            wts = cw_ref[:, cs]
            acc = jnp.zeros((MLSTM_ROW_BLOCK, LANES), F32) + cb_ref[:, cs]
            for k in range(MLSTM_CONV_WIDTH):
                acc = acc + win[lead + k:lead + k + MLSTM_ROW_BLOCK, :] * wts[k:k + 1, :]
            xc_s[pl.ds(r0, MLSTM_ROW_BLOCK), cs] = acc * jax.nn.sigmoid(acc)
        return carry

    lax.fori_loop(0, L // MLSTM_ROW_BLOCK, row_block, 0)
    new_hist = cbuf[L:L + MLSTM_CONV_PAD, :]
    nh_ref[0] = new_hist
    cbuf[0:MLSTM_CONV_PAD, :] = new_hist

    xc = xc_s[...]
    for c in range(di // MXU_TILE):
        cs = slice(c * MXU_TILE, (c + 1) * MXU_TILE)
        qkv_s[:, c * MXU_TILE:(c + 1) * MXU_TILE] = _mm(xc[:, cs], wq_ref[c])
        qkv_s[:, di + c * MXU_TILE:di + (c + 1) * MXU_TILE] = _mm(xc[:, cs], wk_ref[c])
        qkv_s[:, 2 * di + c * MXU_TILE:2 * di + (c + 1) * MXU_TILE] = _mm(xm[:, cs], wv_ref[c])

    gates = _mm_nt(wg_ref[...], qkv_s[...]) + bg_ref[...]
    ig_all = gates[0:nh, :]
    fp = gates[0:SUBLANES, :]
    lf_all = jnp.minimum(fp, 0.0) - jnp.log(1.0 + jnp.exp(-jnp.abs(fp)))
    ri = lax.broadcasted_iota(jnp.int32, (L, L), 0)
    ci = lax.broadcasted_iota(jnp.int32, (L, L), 1)
    causal = ri >= ci
    eye = ri == ci
    tri = jnp.where(ri <= ci, 1.0, 0.0).astype(F32)
    b_all = jnp.dot(lf_all, tri, preferred_element_type=F32, precision=lax.Precision.HIGHEST)

    def to_col(row):
        return jnp.sum(jnp.where(eye, row, 0.0), axis=1, keepdims=True)

    kscale = dh ** -0.5
    for hd in range(nh):
        hs = slice(hd * dh, (hd + 1) * dh)
        qh = qkv_s[:, hd * dh:(hd + 1) * dh]
        kh = qkv_s[:, di + hd * dh:di + (hd + 1) * dh] * kscale
        vh = qkv_s[:, 2 * di + hd * dh:2 * di + (hd + 1) * dh]
        ig_r = ig_all[hd:hd + 1, :]
        b_r = b_all[nh + hd:nh + hd + 1, :]
        b_c = to_col(b_r)
        m_prev = m_ref[0, hd:hd + 1, 0:1]
        n_r = n_ref[0, hd:hd + 1, :]
        cmat = c_ref[0, 0, hd]

        logw = jnp.where(causal, b_c - b_r + ig_r, NEG_INF)
        inter = b_c + m_prev
        m_row = jnp.maximum(jnp.max(logw, axis=1, keepdims=True), inter)
        w_intra = jnp.exp(logw - m_row)
        w_inter = jnp.exp(inter - m_row)
        s = _mm_nt(qh, kh) * w_intra
        num = _mm(s, vh) + w_inter * _mm(qh, cmat)
        den = jnp.sum(s, axis=1, keepdims=True) + w_inter * jnp.sum(qh * n_r, axis=1, keepdims=True)
        hh = num / jnp.maximum(jnp.abs(den), jnp.exp(-m_row))

        mu = jnp.mean(hh, axis=-1, keepdims=True)
        var = jnp.mean(jnp.square(hh - mu), axis=-1, keepdims=True)
        hh_s[:, hs] = (hh - mu) * lax.rsqrt(var + LN_EPS)

        b_end = b_r[:, L - 1:L]
        g_r = b_end - b_r + ig_r
        m_new = jnp.maximum(b_end + m_prev, jnp.max(g_r, axis=1, keepdims=True))
        wk_c = to_col(jnp.exp(g_r - m_new))
        decay = jnp.exp(b_end + m_prev - m_new)
        kw = kh * wk_c
        upd = lax.dot_general(kw.astype(MXU_DT), vh.astype(MXU_DT), (((0,), (0,)), ((), ())),
                              preferred_element_type=F32)
        c_ref[0, 0, hd] = decay * cmat + upd
        n_ref[0, hd:hd + 1, :] = decay * n_r + jnp.sum(kw, axis=0, keepdims=True)
        m_ref[0, hd:hd + 1, :] = jnp.broadcast_to(m_new, (1, LANES))

    hn = hh_s[...] * ng_ref[...]
    out = jax.nn.sigmoid(z) * (hn + sk_ref[...] * xc)
    y_ref[0] = x + _mm(out, wd_ref[...])


def _blockdiag_tiles(w):
    g, bs, _ = w.shape
    per = MXU_TILE // bs
    wt = w.reshape(g // per, per, bs, bs)
    eye = jnp.eye(per, dtype=w.dtype)
    dense = jnp.einsum("tpij,pq->tpiqj", wt, eye)
    return dense.reshape(g // per, MXU_TILE, MXU_TILE)


def _mlstm_kernel_into(c_all_ref, *refs, L):
    del c_all_ref
    _mlstm_kernel(*refs, L=L)


def _mlstm_layer(x, hist, c0, slot_in, c_out, slot_out, n_slots, n0, m0,
                 g, wup, cw, cb, wq, wk, wv, wi, bi, wf, bf, ng, sk, wd, *, chunk):
    B, T, d = x.shape
    di, nh, dh, L = MLSTM_INNER, MLSTM_HEADS, MLSTM_HEAD_DIM, chunk
    assert T % L == 0 and L % MLSTM_ROW_BLOCK == 0
    hist_p = jnp.pad(hist, ((0, 0), (MLSTM_CONV_PAD - MLSTM_CONV_HIST, 0), (0, 0)))
    cw_p = jnp.pad(cw, ((0, SUBLANES - MLSTM_CONV_WIDTH), (0, 0)))
    n0_p = jnp.pad(n0, ((0, 0), (0, SUBLANES - nh), (0, 0)))
    m0_p = jnp.broadcast_to(jnp.pad(m0, ((0, 0), (0, SUBLANES - nh)))[:, :, None], (B, SUBLANES, LANES))
    wg = jnp.pad(jnp.concatenate([wi, wf], axis=1).T, ((0, GATE_ROWS - 2 * nh), (0, 0)))
    bg = jnp.pad(jnp.concatenate([bi, bf]), (0, GATE_ROWS - 2 * nh)).reshape(GATE_ROWS, 1)
    row = lambda v: v.reshape(1, -1)
    nt = di // MXU_TILE
    into = c_out is not None
    y, nhist, c_new, n_new, m_new = pl.pallas_call(
        functools.partial(_mlstm_kernel_into if into else _mlstm_kernel, L=L),
        grid=(B, T // L),
        in_specs=([pl.BlockSpec(memory_space=pl.ANY)] if into else []) + [
            pl.BlockSpec((1, L, d), lambda b, t: (b, t, 0)),
            pl.BlockSpec((1, MLSTM_CONV_PAD, di), lambda b, t: (b, 0, 0)),
            pl.BlockSpec((1, 1, nh, dh, dh), lambda b, t: (slot_in, b, 0, 0, 0)),
            pl.BlockSpec((1, SUBLANES, dh), lambda b, t: (b, 0, 0)),
            pl.BlockSpec((1, SUBLANES, LANES), lambda b, t: (b, 0, 0)),
            _full((1, d)), _full((d, 2 * di)), _full((SUBLANES, di)), _full((1, di)),
            _full((nt, MXU_TILE, MXU_TILE)), _full((nt, MXU_TILE, MXU_TILE)), _full((nt, MXU_TILE, MXU_TILE)),
            _full((GATE_ROWS, 3 * di)), _full((GATE_ROWS, 1)), _full((1, di)), _full((1, di)), _full((di, d)),
        ],
        out_specs=[
            pl.BlockSpec((1, L, d), lambda b, t: (b, t, 0)),
            pl.BlockSpec((1, MLSTM_CONV_PAD, di), lambda b, t: (b, 0, 0)),
            pl.BlockSpec((1, 1, nh, dh, dh), lambda b, t: (slot_out, b, 0, 0, 0)),
            pl.BlockSpec((1, SUBLANES, dh), lambda b, t: (b, 0, 0)),
            pl.BlockSpec((1, SUBLANES, LANES), lambda b, t: (b, 0, 0)),
        ],
        out_shape=[
            jax.ShapeDtypeStruct((B, T, d), F32),
            jax.ShapeDtypeStruct((B, MLSTM_CONV_PAD, di), F32),
            jax.ShapeDtypeStruct((n_slots, B, nh, dh, dh), F32),
            jax.ShapeDtypeStruct((B, SUBLANES, dh), F32),
            jax.ShapeDtypeStruct((B, SUBLANES, LANES), F32),
        ],
        scratch_shapes=[
            pltpu.VMEM((MLSTM_CONV_PAD + L, di), F32),
            pltpu.VMEM((L, di), F32),
            pltpu.VMEM((L, 3 * di), F32),
            pltpu.VMEM((L, di), F32),
        ],
        compiler_params=pltpu.CompilerParams(dimension_semantics=("arbitrary", "arbitrary"),
                                             vmem_limit_bytes=VMEM_LIMIT),
        input_output_aliases={0: 2} if into else {},
        name="mlstm_mixer",
    )(*((c_out,) if into else ()), x, hist_p, c0, n0_p, m0_p, row(g), wup.astype(MXU_DT), cw_p, row(cb),
      _blockdiag_tiles(wq).astype(MXU_DT), _blockdiag_tiles(wk).astype(MXU_DT),
      _blockdiag_tiles(wv).astype(MXU_DT), wg.astype(MXU_DT), bg, row(ng), row(sk), wd.astype(MXU_DT))
    return (y, nhist[:, MLSTM_CONV_PAD - MLSTM_CONV_HIST:, :], c_new, n_new[:, :nh, :], m_new[:, :nh, 0])


def _sorting_network(n):
    pairs = []
    p = 1
    while p < n:
        k = p
        while k >= 1:
            for j in range(k % p, n - k, 2 * k):
                for i in range(min(k, n - j - k)):
                    if (i + j) // (2 * p) == (i + j + k) // (2 * p):
                        pairs.append((i + j, i + j + k))
            k //= 2
        p *= 2
    return pairs


def _top16(tile):
    nrow = PEER_N_KEYS // SUBLANES
    c = [tile[r * SUBLANES:(r + 1) * SUBLANES, :] for r in range(nrow)]
    for i, j in _sorting_network(nrow):
        c[i], c[j] = jnp.maximum(c[i], c[j]), jnp.minimum(c[i], c[j])
    vals = []
    for a in range(PEER_TOPK):
        mx = jnp.max(c[0], axis=0, keepdims=True)
        vals.append(mx)
        if a + 1 < PEER_TOPK:
            hit = c[0] == mx
            for i in range(PEER_TOPK - 1 - a):
                c[i] = jnp.where(hit, c[i + 1], c[i])
    return vals


def _pack_heads(rows):
    sub = lax.broadcasted_iota(jnp.int32, (PEER_HEADS, LANES), 0)
    out = jnp.broadcast_to(rows[0], (PEER_HEADS, LANES))
    for hd in range(1, PEER_HEADS):
        out = jnp.where(sub == hd, rows[hd], out)
    return out


def _route_kernel(x_ref, g_ref, wqt_ref, k1_ref, k2_ref,
                  xt_ref, thr_ref, p1_ref, s2_ref, p2_ref, s1_s, *, tt):
    ng = tt // LANES
    xn = _rms(x_ref[...], g_ref[...])
    xt = xn.T.astype(MXU_DT)
    for gw in range(tt // PEER_WIDE):
        xt_ref[gw] = xt[:, gw * PEER_WIDE:(gw + 1) * PEER_WIDE]
    qt = jnp.dot(wqt_ref[...], xt, preferred_element_type=F32)
    for hd in range(PEER_HEADS):
        q1 = qt[hd * 2 * PEER_HALF:hd * 2 * PEER_HALF + PEER_HALF, :]
        q2 = qt[hd * 2 * PEER_HALF + PEER_HALF:(hd + 1) * 2 * PEER_HALF, :]
        s1 = _mm(k1_ref[hd], q1)
        s2 = _mm(k2_ref[hd], q2)
        for gi in range(ng):
            s1_s[gi, hd] = s1[:, gi * LANES:(gi + 1) * LANES]
            s2_ref[gi, hd] = s2[:, gi * LANES:(gi + 1) * LANES]

    pairs = [(a, b) for a in range(PEER_TOPK) for b in range(PEER_TOPK) if (a + 1) * (b + 1) <= PEER_TOPK]

    def group(gi, carry):
        v1 = [_top16(s1_s[gi, hd]) for hd in range(PEER_HEADS)]
        v2 = [_top16(s2_ref[gi, hd]) for hd in range(PEER_HEADS)]
        V1 = [_pack_heads([v1[hd][a] for hd in range(PEER_HEADS)]) for a in range(PEER_TOPK)]
        V2 = [_pack_heads([v2[hd][a] for hd in range(PEER_HEADS)]) for a in range(PEER_TOPK)]
        cand = [V1[a] + V2[b] for (a, b) in pairs]
        cur = list(cand)
        tau = None
        for it in range(PEER_TOPK):
            mx = cur[0]
            for cnd in cur[1:]:
                mx = jnp.maximum(mx, cnd)
            if it + 1 < PEER_TOPK:
                cur = [jnp.where(cnd == mx, NEG_INF, cnd) for cnd in cur]
            else:
                tau = mx
        E1 = [jnp.exp(V1[a] - V1[0]) for a in range(PEER_TOPK)]
        E2 = [jnp.exp(V2[b] - V2[0]) for b in range(PEER_TOPK)]
        zsum = jnp.zeros((PEER_HEADS, LANES), F32)
        for idx, (a, b) in enumerate(pairs):
            zsum = zsum + jnp.where(cand[idx] >= tau, E1[a] * E2[b], 0.0)
        inv_z = 0.5 / zsum
        T = [None] * PEER_TOPK
        for idx, (a, b) in enumerate(pairs):
            cut = jnp.where(cand[idx] >= tau, V2[b], POS_INF)
            T[a] = cut if T[a] is None else jnp.minimum(T[a], cut)
        for hd in range(PEER_HEADS):
            s1 = s1_s[gi, hd]
            thr = jnp.full((PEER_N_KEYS, LANES), POS_INF, F32)
            for a in range(PEER_TOPK):
                thr = jnp.minimum(thr, jnp.where(s1 >= v1[hd][a], T[a][hd:hd + 1, :], POS_INF))
            thr_ref[gi, hd] = thr
            p1_ref[gi, hd] = jnp.exp(s1 - v1[hd][0]) * inv_z[hd:hd + 1, :]
            p2_ref[gi, hd] = jnp.exp(s2_ref[gi, hd] - v2[hd][0])
        return carry

    lax.fori_loop(0, ng, group, 0)


def _peer_route(x, g, wqt, k1, k2, *, tt):
    n, d = x.shape
    assert n % tt == 0 and tt % PEER_WIDE == 0
    ng = tt // LANES
    tile_spec = pl.BlockSpec((ng, PEER_HEADS, PEER_N_KEYS, LANES), lambda i: (i, 0, 0, 0))
    tile_shape = jax.ShapeDtypeStruct((n // LANES, PEER_HEADS, PEER_N_KEYS, LANES), F32)
    return pl.pallas_call(
        functools.partial(_route_kernel, tt=tt),
        grid=(n // tt,),
        in_specs=[
            pl.BlockSpec((tt, d), lambda i: (i, 0)),
            _full((1, d)), _full((2 * PEER_HALF * PEER_HEADS, d)),
            _full((PEER_HEADS, PEER_N_KEYS, PEER_HALF)), _full((PEER_HEADS, PEER_N_KEYS, PEER_HALF)),
        ],
        out_specs=[pl.BlockSpec((tt // PEER_WIDE, d, PEER_WIDE), lambda i: (i, 0, 0)),
                   tile_spec, tile_spec, tile_spec, tile_spec],
        out_shape=[jax.ShapeDtypeStruct((n // PEER_WIDE, d, PEER_WIDE), MXU_DT),
                   tile_shape, tile_shape, tile_shape, tile_shape],
        scratch_shapes=[pltpu.VMEM((ng, PEER_HEADS, PEER_N_KEYS, LANES), F32)],
        compiler_params=pltpu.CompilerParams(dimension_semantics=("arbitrary",),
                                             vmem_limit_bytes=VMEM_LIMIT),
        name="peer_route",
    )(x, g.reshape(1, d), wqt, k1, k2)


PEER_SUB = 512
PEER_WIDE = MXU_TILE
PEER_KEY_BLOCK = 32
PEER_J_BLOCK = 4
PEER_PIECES = D_MODEL // MXU_TILE
PEER_OUT_ROWS = 512
PEER_BLOCKS_PER_UNIT = ((PEER_WIDE // LANES) * (PEER_SUB // PEER_N_KEYS // PEER_J_BLOCK)
                        * (PEER_N_KEYS // PEER_KEY_BLOCK))


GELU_C0 = 0.7978845608028654
GELU_C1 = 0.044715 * GELU_C0


def _twice_gelu_tanh(x):
    return x * (1.0 + jnp.tanh(x * (GELU_C0 + GELU_C1 * (x * x))))


def _expert_kernel(x_ref, xt_ref, thr_ref, p1_ref, s2_ref, p2_ref, u_ref, vt_ref, u_next_ref, xt_next_ref,
                   gfin_ref, y_ref, acc_s, s_a, s_b, w_a, w_b, *, tt, ec, final_norm):
    s_s = (s_a, s_b)
    w_s = (w_a, w_b)
    c = pl.program_id(1)
    ngw = tt // PEER_WIDE
    halves = PEER_WIDE // LANES
    jn = PEER_SUB // PEER_N_KEYS
    kts = PEER_SUB // MXU_TILE
    nkb = PEER_N_KEYS // PEER_KEY_BLOCK
    njb = jn // PEER_J_BLOCK

    @pl.when(c == 0)
    def _():
        acc_s[...] = jnp.zeros_like(acc_s)

    nsub = ec // PEER_SUB
    assert ngw == 2

    def put_scores(res, buf):
        for half in range(halves):
            s_s[buf][half] = res[:, half * LANES:(half + 1) * LANES]

    def scores(sub, gw, buf):
        e0 = sub * PEER_SUB if isinstance(sub, int) else pl.multiple_of(sub * PEER_SUB, PEER_SUB)
        put_scores(jnp.dot(u_ref[pl.ds(e0, PEER_SUB), :], xt_ref[gw], preferred_element_type=F32), buf)

    def scores_of_next_step():
        put_scores(jnp.dot(u_next_ref[...], xt_next_ref[0], preferred_element_type=F32), 0)

    def outputs(sub, gw, buf):
        rhs = jnp.concatenate([w_s[buf][half] for half in range(halves)], axis=1)
        acc_s[gw] += jnp.dot(vt_ref[sub], rhs, preferred_element_type=F32)

    def gate_block(unit, buf, blk):
        sub, gw = unit
        half = blk // (njb * nkb)
        jj0 = ((blk // nkb) % njb) * PEER_J_BLOCK
        gi = gw * halves + half
        k0 = (blk % nkb) * PEER_KEY_BLOCK
        gate = [None] * PEER_J_BLOCK
        for hd in range(PEER_HEADS):
            s2 = s2_ref[gi, hd, k0:k0 + PEER_KEY_BLOCK, :]
            p2 = p2_ref[gi, hd, k0:k0 + PEER_KEY_BLOCK, :]
            for jo in range(PEER_J_BLOCK):
                j = sub * jn + jj0 + jo
                thr = thr_ref[gi, hd, pl.ds(j, 1), :]
                p1 = p1_ref[gi, hd, pl.ds(j, 1), :]
                term = jnp.where(s2 >= thr, p2, 0.0) * p1
                gate[jo] = term if gate[jo] is None else gate[jo] + term
        for jo in range(PEER_J_BLOCK):
            rows = slice((jj0 + jo) * PEER_N_KEYS + k0, (jj0 + jo) * PEER_N_KEYS + k0 + PEER_KEY_BLOCK)
            act = _twice_gelu_tanh(s_s[buf][half, rows, :])
            w_s[buf][half, rows, :] = (gate[jo] * act).astype(MXU_DT)

    def gates(sub, gw, buf):
        for blk in range(PEER_BLOCKS_PER_UNIT):
            gate_block((sub, gw), buf, blk)

    @pl.when((pl.program_id(0) == 0) & (c == 0))
    def _():
        scores(0, 0, 0)

    w_s[1][...] = jnp.zeros_like(w_s[1])

    def unit_pair(sub, prev_sub, next_scores):
        outputs(prev_sub, 1, 1)
        scores(sub, 1, 1)
        gates(sub, 0, 0)
        outputs(sub, 0, 0)
        next_scores()
        gates(sub, 1, 1)

    def body(sub, carry):
        unit_pair(sub, jnp.maximum(sub - 1, 0), lambda: scores(sub + 1, 0, 0))
        return carry

    assert nsub >= 2
    lax.fori_loop(0, nsub - 1, body, 0)
    unit_pair(nsub - 1, nsub - 2, scores_of_next_step)
    outputs(nsub - 1, 1, 1)

    @pl.when(c == pl.num_programs(1) - 1)
    def _():
        for gw in range(ngw):
            rows = slice(gw * PEER_WIDE, (gw + 1) * PEER_WIDE)
            y = x_ref[rows, :] + acc_s[gw].T
            y_ref[rows, :] = _rms(y, gfin_ref[...]) if final_norm else y


def _peer_experts(x, xt, thr, p1, s2, p2, u4, vt4, g_final, *, tt, ec):
    n, d = x.shape
    ne = u4.shape[0]
    assert n % tt == 0 and tt % PEER_WIDE == 0 and ne % ec == 0 and ec % PEER_SUB == 0
    ng = tt // LANES
    ngw = tt // PEER_WIDE
    jc = ec // PEER_N_KEYS
    halves = PEER_WIDE // LANES
    key_chunk = pl.BlockSpec((ng, PEER_HEADS, jc, LANES), lambda i, c: (i, 0, c, 0))
    all_keys = pl.BlockSpec((ng, PEER_HEADS, PEER_N_KEYS, LANES), lambda i, c: (i, 0, 0, 0))
    ni, nc = n // tt, ne // ec
    subs = ec // PEER_SUB
    next_u = pl.BlockSpec((PEER_SUB, d), lambda i, c: (((c + 1) % nc) * subs, 0))
    next_xt = pl.BlockSpec((1, d, PEER_WIDE), lambda i, c: (jnp.minimum(i + (c + 1) // nc, ni - 1) * ngw, 0, 0))
    return pl.pallas_call(
        functools.partial(_expert_kernel, tt=tt, ec=ec, final_norm=g_final is not None),
        grid=(ni, nc),
        in_specs=[
            pl.BlockSpec((tt, d), lambda i, c: (i, 0)),
            pl.BlockSpec((ngw, d, PEER_WIDE), lambda i, c: (i, 0, 0)),
            key_chunk, key_chunk, all_keys, all_keys,
            pl.BlockSpec((ec, d), lambda i, c: (c, 0)),
            pl.BlockSpec((subs, d, PEER_SUB), lambda i, c: (c, 0, 0)),
            next_u, next_xt, _full((1, d)),
        ],
        out_specs=pl.BlockSpec((tt, d), lambda i, c: (i, 0)),
        out_shape=jax.ShapeDtypeStruct((n, d), F32),
        scratch_shapes=[pltpu.VMEM((ngw, d, PEER_WIDE), F32),
                        pltpu.VMEM((halves, PEER_SUB, LANES), F32), pltpu.VMEM((halves, PEER_SUB, LANES), F32),
                        pltpu.VMEM((halves, PEER_SUB, LANES), MXU_DT),
                        pltpu.VMEM((halves, PEER_SUB, LANES), MXU_DT)],
        compiler_params=pltpu.CompilerParams(dimension_semantics=("arbitrary", "arbitrary"),
                                             vmem_limit_bytes=VMEM_LIMIT),
        name="peer_experts",
    )(x, xt, thr, p1, s2, p2, u4, vt4, u4, xt, (jnp.ones((d,), F32) if g_final is None else g_final).reshape(1, d))


def _peer_weights(wq, k1, k2, u, v):
    ne, d = u.shape
    u4 = u.astype(MXU_DT)
    vt4 = v.astype(MXU_DT).reshape(ne // PEER_SUB, PEER_SUB, d).transpose(0, 2, 1)
    return wq.T.astype(MXU_DT), k1.astype(MXU_DT), k2.astype(MXU_DT), u4, vt4


def _peer_layer(x, g, weights, g_final, *, tt_route, tt, ec):
    wqt, k1, k2, u4, vt4 = weights
    shape = x.shape
    x = x.reshape(-1, shape[-1])
    xt, thr, p1, s2, p2 = _peer_route(x, g, wqt, k1, k2, tt=tt_route)
    return _peer_experts(x, xt, thr, p1, s2, p2, u4, vt4, g_final, tt=tt, ec=ec).reshape(shape)


CONV_PROMPT_TILE = 512
CONV_SAMPLE_SEQS = 8
PEER_ROUTE_TILE = 512
PEER_TOKEN_TILE = 512
PEER_EXPERT_CHUNK = 2048


def kernel(x_prompt, x_sample, cache_conv, state_mlstm_conv, state_C, state_n, state_m, norm_mix, norm_ffn, norm_final, cm_w1, cm_b1, cm_dw, cm_dwb, cm_ln_g, cm_ln_b, cm_w2, cm_b2, ml_wup, ml_convw, ml_convb, ml_wq, ml_wk, ml_wv, ml_wi, ml_bi, ml_wf, ml_bf, ml_norm, ml_skip, ml_wdown, pk_wq, pk_k1, pk_k2, pk_u, pk_v):
    xp, xs = x_prompt, x_sample
    bp, tp, d = xp.shape
    bs, ts, _ = xs.shape
    di, nh, dh = MLSTM_INNER, MLSTM_HEADS, MLSTM_HEAD_DIM
    conv_p, conv_s, mconv_p, mconv_s = [], [], [], []
    np_l, ns_l, mp_l, ms_l = [], [], [], []
    c_p = c_s = None
    for layer in range(DEPTH):
        j = layer // N_MIXERS
        if layer % N_MIXERS == 0:
            cw = (norm_mix[layer], cm_w1[j], cm_b1[j], cm_dw[j], cm_dwb[j], cm_ln_g[j], cm_ln_b[j],
                  cm_w2[j], cm_b2[j])
            xp, hist_p = _conv_layer(xp, jnp.zeros((bp, CONV_HIST, d), F32), *cw, bb=1,
                                     tt=min(tp, CONV_PROMPT_TILE))
            xs, hist_s = _conv_layer(xs, cache_conv[j], *cw, bb=min(bs, CONV_SAMPLE_SEQS), tt=ts)
            conv_p.append(hist_p)
            conv_s.append(hist_s)
        else:
            mw = (norm_mix[layer], ml_wup[j], ml_convw[j], ml_convb[j], ml_wq[j], ml_wk[j], ml_wv[j],
                  ml_wi[j], ml_bi[j], ml_wf[j], ml_bf[j], ml_norm[j], ml_skip[j], ml_wdown[j])
            n_slots = state_C.shape[0]
            xp, hcp, c_p, n_p, m_p = _mlstm_layer(
                xp, jnp.zeros((bp, MLSTM_CONV_HIST, di), F32), jnp.zeros((1, bp, nh, dh, dh), F32), 0,
                c_p, j, n_slots, jnp.zeros((bp, nh, dh), F32), jnp.zeros((bp, nh), F32), *mw,
                chunk=min(tp, MLSTM_PROMPT_CHUNK))
            xs, hcs, c_s, n_s, m_s = _mlstm_layer(
                xs, state_mlstm_conv[j], state_C, j, c_s, j, n_slots, state_n[j], state_m[j], *mw, chunk=ts)
            mconv_p.append(hcp)
            mconv_s.append(hcs)
            np_l.append(n_p)
            ns_l.append(n_s)
            mp_l.append(m_p)
            ms_l.append(m_s)
        pw = _peer_weights(pk_wq[layer], pk_k1[layer], pk_k2[layer], pk_u[layer], pk_v[layer])
        tiles = dict(tt_route=PEER_ROUTE_TILE, tt=PEER_TOKEN_TILE, ec=PEER_EXPERT_CHUNK)
        g_final = norm_final if layer == DEPTH - 1 else None
        xp = _peer_layer(xp, norm_ffn[layer], pw, g_final, **tiles)
        xs = _peer_layer(xs, norm_ffn[layer], pw, g_final, **tiles)
    return (xp, xs,
            jnp.stack(conv_p), jnp.stack(conv_s),
            jnp.stack(mconv_p), jnp.stack(mconv_s),
            c_p, c_s,
            jnp.stack(np_l), jnp.stack(ns_l),
            jnp.stack(mp_l), jnp.stack(ms_l))
```
